```python
import math
import jax
import jax.numpy as jnp
from jax import lax
import numpy as np

D_MODEL = 2048
BATCH = 2
SEQ = 8192
DEPTH = 4

HEAD_DIM = 128
N_HEADS = 4
BRANCH_WIDTH = N_HEADS * HEAD_DIM
N_BRANCHES = 4
DIFF_DIM = HEAD_DIM // 2
IDX_HEADS = 16
IDX_DIM = 64
TOPK_MAX = 256
DILATED_CONFIGS = ((128, 1), (512, 4), (2048, 16))
N_BUCKETS = 32
MAX_DISTANCE = 2048
BLOCK_Q = 128
RMS_EPS = 1e-6
N_BIAS_HEADS = 3 * N_HEADS
QKV_COLS = 3 * BRANCH_WIDTH
IN_SIZES = (N_BRANCHES * BRANCH_WIDTH, N_BRANCHES * D_MODEL, QKV_COLS, IDX_HEADS * IDX_DIM,
            IDX_DIM, IDX_HEADS, QKV_COLS, N_HEADS, QKV_COLS, QKV_COLS)
IN_OFFSETS = tuple(int(o) for o in np.cumsum(IN_SIZES)[:-1])
N_IN = int(sum(IN_SIZES))

kernel_name = 'hybrid_dsa_fox_diff_dilated_gated'


def rms_norm(x, w):
    xf = x.astype(jnp.float32)
    y = xf * lax.rsqrt(jnp.mean(xf * xf, axis=-1, keepdims=True) + RMS_EPS)
    return (y * w.astype(jnp.float32)).astype(x.dtype)


def t5_bucket(dist):
    dist = jnp.maximum(dist, 0)
    max_exact = N_BUCKETS // 2
    d = jnp.maximum(dist, max_exact).astype(jnp.float32)
    large = max_exact + (jnp.log(d / max_exact) / math.log(MAX_DISTANCE / max_exact)
                         * (N_BUCKETS - max_exact)).astype(jnp.int32)
    large = jnp.minimum(large, N_BUCKETS - 1)
    return jnp.where(dist < max_exact, dist, large)


def sweep_blocks(block_fn, seq):
    out = lax.map(block_fn, jnp.arange(seq // BLOCK_Q))
    n, b, q, h, d = out.shape
    return out.transpose(1, 0, 2, 3, 4).reshape(b, n * q, h, d)


def dsa_attention(q, k, v, q_idx, k_idx, w_idx, bias_table):
    seq = q.shape[1]
    topk = min(TOPK_MAX, seq // 4)
    key_pos = jnp.arange(seq)
    gather = jax.vmap(lambda a, i: a[i])
    k_idx_f = k_idx.astype(jnp.float32)

    def block(i):
        t0 = i * BLOCK_Q
        tq = t0 + jnp.arange(BLOCK_Q)
        qi = lax.dynamic_slice_in_dim(q_idx, t0, BLOCK_Q, axis=1).astype(jnp.float32)
        wi = lax.dynamic_slice_in_dim(w_idx, t0, BLOCK_Q, axis=1).astype(jnp.float32) * IDX_HEADS ** -0.5
        rel = jax.nn.relu(jnp.einsum('bqhd,bsd->bqhs', qi, k_idx_f) * IDX_DIM ** -0.5)
        score = jnp.einsum('bqh,bqhs->bqs', wi, rel)
        causal = key_pos[None, :] <= tq[:, None]
        score = jnp.where(causal, score, -jnp.inf)
        _, sel = lax.top_k(score, topk)
        valid = sel <= tq[None, :, None]
        kg = gather(k, sel)
        vg = gather(v, sel)
        qb = lax.dynamic_slice_in_dim(q, t0, BLOCK_Q, axis=1)
        logits = jnp.einsum('bqhd,bqkhd->bhqk', qb, kg).astype(jnp.float32) * HEAD_DIM ** -0.5
        bias = bias_table[t5_bucket(tq[None, :, None] - sel)].transpose(0, 3, 1, 2)
        logits = jnp.where(valid[:, None], logits + bias, -jnp.inf)
        probs = jax.nn.softmax(logits, axis=-1)
        return jnp.einsum('bhqk,bqkhd->bqhd', probs, vg)

    return sweep_blocks(block, seq)


def forgetting_attention(q, k, v, f_logit):
    seq = q.shape[1]
    key_pos = jnp.arange(seq)
    cum = lax.cumsum(jax.nn.log_sigmoid(f_logit.astype(jnp.float32)), axis=1).transpose(0, 2, 1)

    def block(i):
        t0 = i * BLOCK_Q
        tq = t0 + jnp.arange(BLOCK_Q)
        qb = lax.dynamic_slice_in_dim(q, t0, BLOCK_Q, axis=1)
        cq = lax.dynamic_slice_in_dim(cum, t0, BLOCK_Q, axis=2)
        logits = (jnp.einsum('bqhd,bshd->bhqs', qb, k).astype(jnp.float32) * HEAD_DIM ** -0.5
                  + cq[..., None] - cum[:, :, None, :])
        causal = key_pos[None, :] <= tq[:, None]
        probs = jax.nn.softmax(jnp.where(causal, logits, -jnp.inf), axis=-1)
        return jnp.einsum('bhqs,bshd->bqhd', probs, v)

    return sweep_blocks(block, seq)


def differential_attention(q, k, v, lam, subln_w, bias_table, lambda_init):
    seq = q.shape[1]
    key_pos = jnp.arange(seq)

    def block(i):
        t0 = i * BLOCK_Q
        tq = t0 + jnp.arange(BLOCK_Q)
        qb = lax.dynamic_slice_in_dim(q, t0, BLOCK_Q, axis=1)
        logits = jnp.einsum('bqhcd,bshcd->bhcqs', qb, k).astype(jnp.float32) * DIFF_DIM ** -0.5
        bias = bias_table[t5_bucket(tq[:, None] - key_pos[None, :])].transpose(2, 0, 1)
        causal = key_pos[None, :] <= tq[:, None]
        logits = jnp.where(causal, logits + bias[None, :, None], -jnp.inf)
        probs = jax.nn.softmax(logits, axis=-1)
        attn = probs[:, :, 0] - lam * probs[:, :, 1]
        return jnp.einsum('bhqs,bshd->bqhd', attn, v)

    out = sweep_blocks(block, seq)
    return rms_norm(out, subln_w) * (1.0 - lambda_init)


def dilated_attention(q, k, v, bias_table):
    seq = q.shape[1]

    def block(i):
        t0 = i * BLOCK_Q
        tq = t0 + jnp.arange(BLOCK_Q)
        qb = lax.dynamic_slice_in_dim(q, t0, BLOCK_Q, axis=1)
        outs, lses = [], []
        for window, dilation in DILATED_CONFIGS:
            dist = dilation * jnp.arange(window // dilation + 1)
            idx = tq[:, None] - dist[None, :]
            valid = idx >= 0
            idx = jnp.maximum(idx, 0)
            kg = k[:, idx]
            vg = v[:, idx]
            logits = (jnp.einsum('bqhd,bqkhd->bhqk', qb, kg).astype(jnp.float32) * HEAD_DIM ** -0.5
                      + bias_table[t5_bucket(dist)].T[None, :, None, :])
            logits = jnp.where(valid, logits, -jnp.inf)
            m = jnp.max(logits, axis=-1, keepdims=True)
            e = jnp.exp(logits - m)
            s = jnp.sum(e, axis=-1, keepdims=True)
            outs.append(jnp.einsum('bhqk,bqkhd->bqhd', e / s, vg))
            lses.append(m + jnp.log(s))
        wts = jax.nn.softmax(jnp.stack(lses), axis=0).transpose(0, 1, 3, 2, 4)
        return jnp.sum(wts * jnp.stack(outs), axis=0)

    return sweep_blocks(block, seq)


def hybrid_layer(x, layer, norm_w, w_in, fox_b_f, lq1, lk1, lq2, lk2, subln_w, w_branch, w_out, rel_bias):
    bsz, seq, _ = x.shape
    h = rms_norm(x, norm_w)
    p = h @ w_in
    (z, gate, dsa_qkv, idx_q, idx_k, idx_w, fox_qkv, fox_f, diff_qkv, dil_qkv) = jnp.split(p, IN_OFFSETS, axis=-1)

    def heads3(t):
        return [a.reshape(bsz, seq, N_HEADS, HEAD_DIM) for a in jnp.split(t, 3, axis=-1)]

    bias_dsa, bias_diff, bias_dil = jnp.split(rel_bias, 3, axis=-1)

    q, k, v = heads3(dsa_qkv)
    o_dsa = dsa_attention(q, k, v, idx_q.reshape(bsz, seq, IDX_HEADS, IDX_DIM), idx_k, idx_w, bias_dsa)

    q, k, v = heads3(fox_qkv)
    o_fox = forgetting_attention(q, k, v, fox_f + fox_b_f)

    q, k, v = heads3(diff_qkv)
    q = q.reshape(bsz, seq, N_HEADS, 2, DIFF_DIM)
    k = k.reshape(bsz, seq, N_HEADS, 2, DIFF_DIM)
    lambda_init = 0.8 - 0.6 * math.exp(-0.3 * layer)
    lam = (jnp.exp(jnp.sum((lq1 * lk1).astype(jnp.float32)))
           - jnp.exp(jnp.sum((lq2 * lk2).astype(jnp.float32))) + lambda_init)
    o_diff = differential_attention(q, k, v, lam, subln_w, bias_diff, lambda_init)

    q, k, v = heads3(dil_qkv)
    o_dil = dilated_attention(q, k, v, bias_dil)

    branches = (o_dsa, o_fox, o_diff, o_dil)
    z_parts = jnp.split(z, N_BRANCHES, axis=-1)
    g_parts = jnp.split(gate, N_BRANCHES, axis=-1)
    merged = None
    for b in range(N_BRANCHES):
        y = branches[b].reshape(bsz, seq, BRANCH_WIDTH).astype(x.dtype) * jax.nn.silu(z_parts[b])
        term = jax.nn.sigmoid(g_parts[b]) * (y @ w_branch[b])
        merged = term if merged is None else merged + term
    return (x + merged @ w_out).astype(x.dtype)


def setup_inputs(seed: int = 0) -> dict:
    key = jax.random.key(seed)
    ks = jax.random.split(key, 13)

    def nrm(k, shape, scale):
        return scale * jax.random.normal(k, shape, jnp.float32)

    return {
        'x': nrm(ks[0], (BATCH, SEQ, D_MODEL), 1.0),
        'norm_w': 1.0 + nrm(ks[1], (DEPTH, D_MODEL), 0.05),
        'w_in': nrm(ks[2], (DEPTH, D_MODEL, N_IN), D_MODEL ** -0.5),
        'fox_b_f': 4.0 + nrm(ks[3], (DEPTH, N_HEADS), 1.0),
        'diff_lq1': nrm(ks[4], (DEPTH, DIFF_DIM), 0.1),
        'diff_lk1': nrm(ks[5], (DEPTH, DIFF_DIM), 0.1),
        'diff_lq2': nrm(ks[6], (DEPTH, DIFF_DIM), 0.1),
        'diff_lk2': nrm(ks[7], (DEPTH, DIFF_DIM), 0.1),
        'diff_subln_w': 1.0 + nrm(ks[8], (DEPTH, HEAD_DIM), 0.05),
        'w_branch': nrm(ks[9], (DEPTH, N_BRANCHES, BRANCH_WIDTH, D_MODEL), BRANCH_WIDTH ** -0.5),
        'w_out': nrm(ks[10], (DEPTH, D_MODEL, D_MODEL), D_MODEL ** -0.5),
        'rel_bias': nrm(ks[11], (N_BUCKETS, N_BIAS_HEADS), 0.5),
        'final_norm_w': 1.0 + nrm(ks[12], (D_MODEL,), 0.05),
    }


def reference(x, norm_w, w_in, fox_b_f, diff_lq1, diff_lk1, diff_lq2, diff_lk2, diff_subln_w,
              w_branch, w_out, rel_bias, final_norm_w):
    for layer in range(DEPTH):
        x = hybrid_layer(x, layer, norm_w[layer], w_in[layer], fox_b_f[layer],
                         diff_lq1[layer], diff_lk1[layer], diff_lq2[layer], diff_lk2[layer],
                         diff_subln_w[layer], w_branch[layer], w_out[layer], rel_bias)
    return rms_norm(x, final_norm_w)
```

```python
import functools
import math

import jax
import jax.numpy as jnp
import numpy as np
from jax import lax
from jax.experimental import pallas as pl
from jax.experimental.pallas import tpu as pltpu

F32 = jnp.float32
BF16 = jnp.bfloat16

D_MODEL = 2048
DEPTH = 4
HEAD_DIM = 128
N_HEADS = 4
BRANCH_WIDTH = N_HEADS * HEAD_DIM
N_BRANCHES = 4
DIFF_DIM = HEAD_DIM // 2
IDX_HEADS = 16
IDX_DIM = 64
TOPK_MAX = 256
DILATED_CONFIGS = ((128, 1), (512, 4), (2048, 16))
N_BUCKETS = 32
MAX_DISTANCE = 2048
RMS_EPS = 1e-6

LANES = 128
TQ = 256
NEG = -1e30
VMEM_LIMIT = 56 * 1024 * 1024

COL_IDXQ = 0
COL_A = 1024
COL_B = COL_A + 3 * BRANCH_WIDTH
COL_C = COL_B + 3 * BRANCH_WIDTH
COL_D = COL_C + 3 * BRANCH_WIDTH
N_PACK = COL_D + 3 * BRANCH_WIDTH
N_ZG = N_BRANCHES * BRANCH_WIDTH + N_BRANCHES * D_MODEL
N_MISC = 256
MISC_W = 2 * IDX_DIM
MISC_F = MISC_W + IDX_HEADS

_NT = (((1,), (1,)), ((), ()))


def _cparams(sem):
    return pltpu.CompilerParams(dimension_semantics=sem, vmem_limit_bytes=VMEM_LIMIT)


def _rmsnorm_kernel(x_ref, w_ref, o_ref):
    x = x_ref[...]
    y = x * lax.rsqrt(jnp.mean(x * x, axis=-1, keepdims=True) + RMS_EPS)
    o_ref[...] = (y * w_ref[...]).astype(o_ref.dtype)


def _rmsnorm(x2, w, out_dtype, tm=512):
    m, d = x2.shape
    return pl.pallas_call(
        _rmsnorm_kernel,
        grid=(m // tm,),
        in_specs=[pl.BlockSpec((tm, d), lambda i: (i, 0)), pl.BlockSpec((1, d), lambda i: (0, 0))],
        out_specs=pl.BlockSpec((tm, d), lambda i: (i, 0)),
        out_shape=jax.ShapeDtypeStruct((m, d), out_dtype),
        compiler_params=_cparams(("parallel",)),
        name="rmsnorm",
    )(x2, w.reshape(1, d))


def _matmul_kernel(a_ref, b_ref, o_ref):
    o_ref[...] = jnp.dot(a_ref[...], b_ref[...], preferred_element_type=F32).astype(o_ref.dtype)


def _matmul_res_kernel(a_ref, b_ref, r_ref, o_ref):
    o_ref[...] = r_ref[...] + jnp.dot(a_ref[...], b_ref[...], preferred_element_type=F32)


def _matmul(a, b, out_dtype, tm, tn, residual=None, name="matmul"):
    m, k = a.shape
    n = b.shape[1]
    tm, tn = min(tm, m), min(tn, n)
    in_specs = [pl.BlockSpec((tm, k), lambda i, j: (i, 0)), pl.BlockSpec((k, tn), lambda i, j: (0, j))]
    args = [a, b]
    kern = _matmul_kernel
    if residual is not None:
        in_specs.append(pl.BlockSpec((tm, tn), lambda i, j: (i, j)))
        args.append(residual)
        kern = _matmul_res_kernel
    return pl.pallas_call(
        kern,
        grid=(m // tm, n // tn),
        in_specs=in_specs,
        out_specs=pl.BlockSpec((tm, tn), lambda i, j: (i, j)),
        out_shape=jax.ShapeDtypeStruct((m, n), out_dtype),
        compiler_params=_cparams(("parallel", "arbitrary")),
        name=name,
    )(*args)


def _t5_bucket(dist):
    dist = jnp.maximum(dist, 0)
    max_exact = N_BUCKETS // 2
    d = jnp.maximum(dist, max_exact).astype(F32)
    large = max_exact + (jnp.log(d / max_exact) / math.log(MAX_DISTANCE / max_exact)
                         * (N_BUCKETS - max_exact)).astype(jnp.int32)
    large = jnp.minimum(large, N_BUCKETS - 1)
    return jnp.where(dist < max_exact, dist, large)


def _tile_distances(first, n_tiles):
    i = jnp.arange(LANES)[None, :, None]
    j = jnp.arange(LANES)[None, None, :]
    o = (jnp.arange(n_tiles) + first)[:, None, None]
    return o * LANES + i - j


def _bias_tiles(table, first, n_tiles, causal):
    d = _tile_distances(first, n_tiles)
    t = jnp.transpose(table[_t5_bucket(d)], (3, 0, 1, 2))
    if causal:
        t = jnp.where(d[None] >= 0, t, NEG)
    return t


def _dilated_tiles(table, first, n_tiles):
    d = _tile_distances(first, n_tiles)
    mult = jnp.zeros(d.shape, F32)
    for window, dilation in DILATED_CONFIGS:
        mult = mult + ((d >= 0) & (d <= window) & (d % dilation == 0)).astype(F32)
    t = jnp.transpose(table[_t5_bucket(d)], (3, 0, 1, 2)) + jnp.log(jnp.maximum(mult, 1.0))[None]
    return jnp.where(mult[None] > 0, t, NEG)


def _bias256(tab_ref, h, a, first, n_tiles):
    def tile(off):
        return tab_ref[h, jnp.clip(off - first, 0, n_tiles - 1)]
    top = jnp.concatenate([tile(a), tile(a - 1)], axis=1)
    bot = jnp.concatenate([tile(a + 1), tile(a)], axis=1)
    return jnp.concatenate([top, bot], axis=0)


def _softmax_step(s, v, m_ref, l_ref, acc_ref, idx):
    m_prev = m_ref[idx]
    l_prev = l_ref[idx]
    m_cur = jnp.max(s, axis=1, keepdims=True)
    m_next = jnp.maximum(m_prev, m_cur)
    alpha = jnp.exp(m_prev - m_next)
    p = jnp.exp(s - jnp.tile(m_next, (1, s.shape[1] // LANES)))
    l_ref[idx] = alpha * l_prev + jnp.sum(p, axis=1, keepdims=True)
    m_ref[idx] = m_next
    acc_ref[idx] = acc_ref[idx] * alpha + jnp.dot(p.astype(BF16), v, preferred_element_type=F32)


def _init_softmax(m_ref, l_ref, acc_ref):
    m_ref[...] = jnp.full(m_ref.shape, NEG, F32)
    l_ref[...] = jnp.zeros(l_ref.shape, F32)
    acc_ref[...] = jnp.zeros(acc_ref.shape, F32)


def _softmax_scratch(n):
    return [pltpu.VMEM((n, TQ, LANES), F32), pltpu.VMEM((n, TQ, LANES), F32),
            pltpu.VMEM((n, TQ, HEAD_DIM), F32)]


def _head(ref, h):
    return ref[0, :, h * HEAD_DIM:(h + 1) * HEAD_DIM]


def _logsig_cumsum_kernel(f_ref, b_ref, o_ref):
    x = f_ref[0, 0] + b_ref[0, 0]
    ls = jnp.minimum(x, 0.0) - jnp.log1p(jnp.exp(-jnp.abs(x)))
    rows = ls.shape[0]
    r = lax.broadcasted_iota(jnp.int32, (LANES, LANES), 0)
    c = lax.broadcasted_iota(jnp.int32, (LANES, LANES), 1)
    upper = (r <= c).astype(F32)
    within = jnp.dot(ls, upper, preferred_element_type=F32, precision=lax.Precision.HIGHEST)
    total = jnp.broadcast_to(within[:, LANES - 1:LANES], (rows, LANES))
    rr = lax.broadcasted_iota(jnp.int32, (rows, rows), 0)
    rc = lax.broadcasted_iota(jnp.int32, (rows, rows), 1)
    strict = (rc < rr).astype(F32)
    before = jnp.dot(strict, total, preferred_element_type=F32, precision=lax.Precision.HIGHEST)
    o_ref[0, 0] = within + before


def _fox_cum(f_bhs, b_f):
    bsz, nh, seq = f_bhs.shape
    rows = seq // LANES
    out = pl.pallas_call(
        _logsig_cumsum_kernel,
        grid=(bsz, nh),
        in_specs=[pl.BlockSpec((1, 1, rows, LANES), lambda b, h: (b, h, 0, 0)),
                  pl.BlockSpec((1, 1, 1, LANES), lambda b, h: (0, h, 0, 0))],
        out_specs=pl.BlockSpec((1, 1, rows, LANES), lambda b, h: (b, h, 0, 0)),
        out_shape=jax.ShapeDtypeStruct((bsz, nh, rows, LANES), F32),
        compiler_params=_cparams(("parallel", "parallel")),
        name="fox_cumsum",
    )(f_bhs.reshape(bsz, nh, rows, LANES), jnp.broadcast_to(b_f[None, :, None, None], (1, nh, 1, LANES)))
    return out.reshape(bsz, nh, seq)


def _fox_kernel(q_ref, k_ref, v_ref, cq_ref, ck_ref, o_ref, m_ref, l_ref, acc_ref):
    qi, ki = pl.program_id(1), pl.program_id(2)

    @pl.when(ki == 0)
    def _():
        _init_softmax(m_ref, l_ref, acc_ref)

    def step(diagonal):
        if diagonal:
            row = lax.broadcasted_iota(jnp.int32, (TQ, TQ), 0)
            col = lax.broadcasted_iota(jnp.int32, (TQ, TQ), 1)
            causal = col <= row
        for h in range(N_HEADS):
            s = lax.dot_general(_head(q_ref, h), _head(k_ref, h), _NT, preferred_element_type=F32)
            s = s * HEAD_DIM ** -0.5 + cq_ref[0, :, h:h + 1] - ck_ref[0, h:h + 1, :]
            if diagonal:
                s = jnp.where(causal, s, NEG)
            _softmax_step(s, _head(v_ref, h), m_ref, l_ref, acc_ref, h)

    @pl.when(ki < qi)
    def _():
        step(False)

    @pl.when(ki == qi)
    def _():
        step(True)
        for h in range(N_HEADS):
            o_ref[0, :, h * HEAD_DIM:(h + 1) * HEAD_DIM] = acc_ref[h] / l_ref[h]


def _fox_attention(pack, cum_bsh, cum_bhs):
    bsz, seq, _ = pack.shape
    nq = seq // TQ
    cb = COL_B // BRANCH_WIDTH

    def kv_map(off):
        return lambda b, i, j: (b, jnp.minimum(i, j), cb + off)

    return pl.pallas_call(
        _fox_kernel,
        grid=(bsz, nq, nq),
        in_specs=[pl.BlockSpec((1, TQ, BRANCH_WIDTH), lambda b, i, j: (b, i, cb)),
                  pl.BlockSpec((1, TQ, BRANCH_WIDTH), kv_map(1)),
                  pl.BlockSpec((1, TQ, BRANCH_WIDTH), kv_map(2)),
                  pl.BlockSpec((1, TQ, N_HEADS), lambda b, i, j: (b, i, 0)),
                  pl.BlockSpec((1, N_HEADS, TQ), lambda b, i, j: (b, 0, jnp.minimum(i, j)))],
        out_specs=pl.BlockSpec((1, TQ, BRANCH_WIDTH), lambda b, i, j: (b, i, 0)),
        out_shape=jax.ShapeDtypeStruct((bsz, seq, BRANCH_WIDTH), F32),
        scratch_shapes=_softmax_scratch(N_HEADS),
        compiler_params=_cparams(("parallel", "parallel", "arbitrary")),
        name="fox_attention",
    )(pack, pack, pack, cum_bsh, cum_bhs)


DIFF_FIRST, DIFF_TILES = -1, 15


def _diff_kernel(lambda_init, q_ref, k_ref, v_ref, tab_ref, lqk_ref, sub_ref, o_ref, m_ref, l_ref, acc_ref):
    qi, ki = pl.program_id(1), pl.program_id(2)

    @pl.when(ki == 0)
    def _():
        _init_softmax(m_ref, l_ref, acc_ref)

    @pl.when(ki <= qi)
    def _():
        lane = lax.broadcasted_iota(jnp.int32, (TQ, HEAD_DIM), 1)
        for h in range(N_HEADS):
            q = _head(q_ref, h)
            k = _head(k_ref, h)
            v = _head(v_ref, h)
            bias = _bias256(tab_ref, h, 2 * (qi - ki), DIFF_FIRST, DIFF_TILES)
            for c in range(2):
                keep = lane < DIFF_DIM if c == 0 else lane >= DIFF_DIM
                qc = jnp.where(keep, q, jnp.zeros_like(q))
                s = lax.dot_general(qc, k, _NT, preferred_element_type=F32) * DIFF_DIM ** -0.5 + bias
                _softmax_step(s, v, m_ref, l_ref, acc_ref, 2 * h + c)

    @pl.when(ki == qi)
    def _():
        lqk = lqk_ref[...]
        lam = (jnp.exp(jnp.sum(lqk[0:1] * lqk[1:2], axis=1, keepdims=True))
               - jnp.exp(jnp.sum(lqk[2:3] * lqk[3:4], axis=1, keepdims=True)) + lambda_init)
        for h in range(N_HEADS):
            o = acc_ref[2 * h] / l_ref[2 * h] - lam * (acc_ref[2 * h + 1] / l_ref[2 * h + 1])
            y = o * lax.rsqrt(jnp.mean(o * o, axis=-1, keepdims=True) + RMS_EPS) * sub_ref[...]
            o_ref[0, :, h * HEAD_DIM:(h + 1) * HEAD_DIM] = y * (1.0 - lambda_init)


def _diff_attention(pack, tab, lqk, subln_w, lambda_init):
    bsz, seq, _ = pack.shape
    nq = seq // TQ
    cb = COL_C // BRANCH_WIDTH

    def kv_map(off):
        return lambda b, i, j: (b, jnp.minimum(i, j), cb + off)

    return pl.pallas_call(
        functools.partial(_diff_kernel, lambda_init),
        grid=(bsz, nq, nq),
        in_specs=[pl.BlockSpec((1, TQ, BRANCH_WIDTH), lambda b, i, j: (b, i, cb)),
                  pl.BlockSpec((1, TQ, BRANCH_WIDTH), kv_map(1)),
                  pl.BlockSpec((1, TQ, BRANCH_WIDTH), kv_map(2)),
                  pl.BlockSpec(tab.shape, lambda b, i, j: (0, 0, 0, 0)),
                  pl.BlockSpec(lqk.shape, lambda b, i, j: (0, 0)),
                  pl.BlockSpec((1, HEAD_DIM), lambda b, i, j: (0, 0))],
        out_specs=pl.BlockSpec((1, TQ, BRANCH_WIDTH), lambda b, i, j: (b, i, 0)),
        out_shape=jax.ShapeDtypeStruct((bsz, seq, BRANCH_WIDTH), F32),
        scratch_shapes=_softmax_scratch(2 * N_HEADS),
        compiler_params=_cparams(("parallel", "parallel", "arbitrary")),
        name="diff_attention",
    )(pack, pack, pack, tab, lqk, subln_w.reshape(1, HEAD_DIM))


DIL_SPAN = MAX_DISTANCE // TQ + 1
DIL_FIRST, DIL_TILES = -1, 2 * DIL_SPAN + 1


def _dil_kernel(q_ref, k_ref, v_ref, tab_ref, o_ref, m_ref, l_ref, acc_ref):
    qi, r = pl.program_id(1), pl.program_id(2)
    back = DIL_SPAN - 1 - r

    @pl.when(r == 0)
    def _():
        _init_softmax(m_ref, l_ref, acc_ref)

    @pl.when(qi - back >= 0)
    def _():
        for h in range(N_HEADS):
            s = lax.dot_general(_head(q_ref, h), _head(k_ref, h), _NT, preferred_element_type=F32)
            s = s * HEAD_DIM ** -0.5 + _bias256(tab_ref, h, 2 * back, DIL_FIRST, DIL_TILES)
            _softmax_step(s, _head(v_ref, h), m_ref, l_ref, acc_ref, h)

    @pl.when(r == DIL_SPAN - 1)
    def _():
        for h in range(N_HEADS):
            o_ref[0, :, h * HEAD_DIM:(h + 1) * HEAD_DIM] = acc_ref[h] / l_ref[h]


def _dil_attention(pack, tab):
    bsz, seq, _ = pack.shape
    nq = seq // TQ
    cb = COL_D // BRANCH_WIDTH

    def kv_map(off):
        return lambda b, i, r: (b, jnp.maximum(i - (DIL_SPAN - 1 - r), 0), cb + off)

    return pl.pallas_call(
        _dil_kernel,
        grid=(bsz, nq, DIL_SPAN),
        in_specs=[pl.BlockSpec((1, TQ, BRANCH_WIDTH), lambda b, i, r: (b, i, cb)),
                  pl.BlockSpec((1, TQ, BRANCH_WIDTH), kv_map(1)),
                  pl.BlockSpec((1, TQ, BRANCH_WIDTH), kv_map(2)),
                  pl.BlockSpec(tab.shape, lambda b, i, r: (0, 0, 0, 0))],
        out_specs=pl.BlockSpec((1, TQ, BRANCH_WIDTH), lambda b, i, r: (b, i, 0)),
        out_shape=jax.ShapeDtypeStruct((bsz, seq, BRANCH_WIDTH), F32),
        scratch_shapes=_softmax_scratch(N_HEADS),
        compiler_params=_cparams(("parallel", "parallel", "arbitrary")),
        name="dilated_attention",
    )(pack, pack, pack, tab)


DSA_FIRST, DSA_TILES = 0, 14
SEL_ROWS = 128
LOWEST = -3.0e38


def _dsa_kernel(topk, iq_ref, w_ref, k2_ref, q_ref, k_ref, v_ref, tab_ref, o_ref,
                sc_ref, thr_ref, m_ref, l_ref, acc_ref):
    qi = pl.program_id(1)
    n_tiles = qi + 1
    lane = lax.broadcasted_iota(jnp.int32, (TQ, 2 * IDX_DIM), 1)
    w = w_ref[0] * (IDX_HEADS ** -0.5 * IDX_DIM ** -0.5)

    def score_tile(kt, diagonal):
        start = pl.multiple_of(kt * TQ, TQ)
        k2 = k2_ref[0, pl.ds(start, TQ), :]
        acc = jnp.zeros((TQ, TQ), F32)
        for pair in range(IDX_HEADS // 2):
            qp = iq_ref[0, :, pair * 2 * IDX_DIM:(pair + 1) * 2 * IDX_DIM]
            for c in range(2):
                keep = lane < IDX_DIM if c == 0 else lane >= IDX_DIM
                qc = jnp.where(keep, qp, jnp.zeros_like(qp))
                rel = jnp.maximum(lax.dot_general(qc, k2, _NT, preferred_element_type=F32), 0.0)
                hh = 2 * pair + c
                acc = acc + w[:, hh:hh + 1] * rel
        if diagonal:
            row = lax.broadcasted_iota(jnp.int32, (TQ, TQ), 0)
            col = lax.broadcasted_iota(jnp.int32, (TQ, TQ), 1)
            acc = jnp.where(col <= row, acc, -jnp.inf)
        sc_ref[:, pl.ds(start, TQ)] = acc

    def score_body(kt, carry):
        score_tile(kt, False)
        return carry

    lax.fori_loop(0, qi, score_body, 0)
    score_tile(qi, True)

    n_chunks = n_tiles * (TQ // LANES)
    for half in range(TQ // SEL_ROWS):
        rows = pl.ds(half * SEL_ROWS, SEL_ROWS)
        t_row = qi * TQ + half * SEL_ROWS + lax.broadcasted_iota(jnp.int32, (SEL_ROWS, 1), 0)
        few = t_row < topk

        def reduce_cols(fn, init, combine):
            def body(c, carry):
                blk = sc_ref[rows, pl.ds(pl.multiple_of(c * LANES, LANES), LANES)]
                return combine(carry, fn(blk, c))
            return lax.fori_loop(0, n_chunks, body, init)

        def count(pred):
            cnt = reduce_cols(lambda blk, c: jnp.where(pred(blk, c), 1.0, 0.0),
                              jnp.zeros((SEL_ROWS, LANES), F32), lambda a, b: a + b)
            return jnp.sum(cnt, axis=1, keepdims=True)

        row_max = jnp.max(reduce_cols(lambda blk, c: blk, jnp.full((SEL_ROWS, LANES), -jnp.inf, F32),
                                      jnp.maximum), axis=1, keepdims=True)
        row_min = jnp.min(reduce_cols(lambda blk, c: jnp.where(blk == -jnp.inf, jnp.inf, blk),
                                      jnp.full((SEL_ROWS, LANES), jnp.inf, F32), jnp.minimum),
                          axis=1, keepdims=True)
        hi0 = row_max + (jnp.abs(row_max) * 1e-6 + 1e-30)
        kf = float(topk)

        def cond(st):
            return jnp.logical_and(st[0] < 400, jnp.min(st[4]) < 0.5)

        def body(st):
            it, lo, hi, thr, done, tie = st
            mid = lo + (hi - lo) * 0.5
            cnt = count(lambda blk, c: blk >= mid)
            live = done < 0.5
            exact = jnp.logical_and(live, cnt == kf)
            stuck = jnp.logical_and(jnp.logical_and(live, cnt != kf), jnp.logical_or(mid <= lo, mid >= hi))
            thr = jnp.where(exact, mid, jnp.where(stuck, lo, thr))
            tie = jnp.where(stuck, 1.0, tie)
            done = jnp.where(jnp.logical_or(exact, stuck), 1.0, done)
            lo = jnp.where(cnt > kf, mid, lo)
            hi = jnp.where(cnt < kf, mid, hi)
            return it + 1, lo, hi, thr, done, tie

        zeros = jnp.zeros((SEL_ROWS, 1), F32)
        init = (jnp.int32(0), row_min, hi0, jnp.full((SEL_ROWS, 1), LOWEST, F32),
                jnp.where(few, 1.0, 0.0), zeros)
        _, _, _, thr, _, tie = lax.while_loop(cond, body, init)
        thr_ref[rows, :] = jnp.broadcast_to(thr, (SEL_ROWS, LANES))

        @pl.when(jnp.max(tie) > 0.5)
        def _():
            is_tie = tie > 0.5
            need = kf - count(lambda blk, c: blk > thr)
            lane_i = lax.broadcasted_iota(jnp.int32, (SEL_ROWS, LANES), 1)

            def kept(limit):
                return count(lambda blk, c: jnp.logical_and(blk == thr, c * LANES + lane_i <= limit))

            def jbody(_, st):
                jlo, jhi = st
                jmid = (jlo + jhi) // 2
                ok = kept(jmid) >= need
                return jnp.where(ok, jlo, jmid), jnp.where(ok, jmid, jhi)

            n_cols = n_chunks * LANES
            jlo0 = jnp.full((SEL_ROWS, 1), -1, jnp.int32)
            jhi0 = jnp.zeros((SEL_ROWS, 1), jnp.int32) + (n_cols - 1)
            _, cut = lax.fori_loop(0, 14, jbody, (jlo0, jhi0))

            def drop(c, carry):
                cols = pl.ds(pl.multiple_of(c * LANES, LANES), LANES)
                blk = sc_ref[rows, cols]
                extra = jnp.logical_and(jnp.logical_and(is_tie, blk == thr), c * LANES + lane_i > cut)
                sc_ref[rows, cols] = jnp.where(extra, -jnp.inf, blk)
                return carry

            lax.fori_loop(0, n_chunks, drop, 0)

    _init_softmax(m_ref, l_ref, acc_ref)

    def attend(kt, carry):
        start = pl.multiple_of(kt * TQ, TQ)
        sel = sc_ref[:, pl.ds(start, TQ)] >= jnp.tile(thr_ref[...], (1, TQ // LANES))
        for h in range(N_HEADS):
            hs = slice(h * HEAD_DIM, (h + 1) * HEAD_DIM)
            k = k_ref[0, pl.ds(start, TQ), hs]
            v = v_ref[0, pl.ds(start, TQ), hs]
            s = lax.dot_general(_head(q_ref, h), k, _NT, preferred_element_type=F32) * HEAD_DIM ** -0.5
            s = s + _bias256(tab_ref, h, 2 * (qi - kt), DSA_FIRST, DSA_TILES)
            _softmax_step(jnp.where(sel, s, NEG), v, m_ref, l_ref, acc_ref, h)
        return carry

    lax.fori_loop(0, n_tiles, attend, 0)
    for h in range(N_HEADS):
        o_ref[0, :, h * HEAD_DIM:(h + 1) * HEAD_DIM] = acc_ref[h] / l_ref[h]


def _dsa_attention(pack, w_idx, k2, tab):
    bsz, seq, _ = pack.shape
    nq = seq // TQ
    topk = min(TOPK_MAX, seq // 4)
    ca = COL_A // BRANCH_WIDTH

    def whole(col):
        return pl.BlockSpec((1, seq, BRANCH_WIDTH), lambda b, i: (b, 0, col))

    return pl.pallas_call(
        functools.partial(_dsa_kernel, topk),
        grid=(bsz, nq),
        in_specs=[pl.BlockSpec((1, TQ, IDX_HEADS * IDX_DIM), lambda b, i: (b, i, 0)),
                  pl.BlockSpec((1, TQ, IDX_HEADS), lambda b, i: (b, i, 0)),
                  pl.BlockSpec((1, seq, 2 * IDX_DIM), lambda b, i: (b, 0, 0)),
                  pl.BlockSpec((1, TQ, BRANCH_WIDTH), lambda b, i: (b, i, ca)),
                  whole(ca + 1), whole(ca + 2),
                  pl.BlockSpec(tab.shape, lambda b, i: (0, 0, 0, 0))],
        out_specs=pl.BlockSpec((1, TQ, BRANCH_WIDTH), lambda b, i: (b, i, 0)),
        out_shape=jax.ShapeDtypeStruct((bsz, seq, BRANCH_WIDTH), F32),
        scratch_shapes=[pltpu.VMEM((TQ, seq), F32), pltpu.VMEM((TQ, LANES), F32)] + _softmax_scratch(N_HEADS),
        compiler_params=_cparams(("parallel", "arbitrary")),
        name="dsa_attention",
    )(pack, w_idx, k2, pack, pack, pack, tab)


def _merge_kernel(oa_ref, ob_ref, oc_ref, od_ref, z_ref, g0_ref, g1_ref, g2_ref, g3_ref, wb_ref, o_ref):
    merged = None
    for b, (o_ref_b, g_ref) in enumerate(zip((oa_ref, ob_ref, oc_ref, od_ref),
                                             (g0_ref, g1_ref, g2_ref, g3_ref))):
        z = z_ref[:, b * BRANCH_WIDTH:(b + 1) * BRANCH_WIDTH]
        y = o_ref_b[...] * (z * jax.nn.sigmoid(z))
        term = jax.nn.sigmoid(g_ref[...]) * jnp.dot(y.astype(BF16), wb_ref[b], preferred_element_type=F32)
        merged = term if merged is None else merged + term
    o_ref[...] = merged.astype(o_ref.dtype)


def _merge(branches, zg, w_branch, tm=256):
    m = zg.shape[0]
    o_spec = pl.BlockSpec((tm, BRANCH_WIDTH), lambda i: (i, 0))
    g_specs = [pl.BlockSpec((tm, D_MODEL), functools.partial(lambda i, b: (i, 1 + b), b=b))
               for b in range(N_BRANCHES)]
    return pl.pallas_call(
        _merge_kernel,
        grid=(m // tm,),
        in_specs=[o_spec] * N_BRANCHES + [pl.BlockSpec((tm, N_BRANCHES * BRANCH_WIDTH), lambda i: (i, 0))]
        + g_specs + [pl.BlockSpec(w_branch.shape, lambda i: (0, 0, 0))],
        out_specs=pl.BlockSpec((tm, D_MODEL), lambda i: (i, 0)),
        out_shape=jax.ShapeDtypeStruct((m, D_MODEL), BF16),
        compiler_params=_cparams(("parallel",)),
        name="gated_merge",
    )(*branches, zg, zg, zg, zg, zg, w_branch)


def _pack_w_in(w_in):
    sizes = (N_BRANCHES * BRANCH_WIDTH, N_BRANCHES * D_MODEL, 3 * BRANCH_WIDTH, IDX_HEADS * IDX_DIM,
             IDX_DIM, IDX_HEADS, 3 * BRANCH_WIDTH, N_HEADS, 3 * BRANCH_WIDTH, 3 * BRANCH_WIDTH)
    offs = np.concatenate([[0], np.cumsum(sizes)])
    z, gate, a, iq, ik, iw, b, ff, c, d = [w_in[:, :, offs[n]:offs[n + 1]] for n in range(len(sizes))]
    pad = jnp.zeros(w_in.shape[:2] + (N_MISC - MISC_F - N_HEADS,), w_in.dtype)
    return (jnp.concatenate([iq, a, b, c, d], axis=-1).astype(BF16),
            jnp.concatenate([z, gate], axis=-1).astype(BF16),
            jnp.concatenate([ik, ik, iw, ff, pad], axis=-1).astype(BF16))


def _layer(x2, bsz, seq, layer, norm_w, w_pack, w_zg, w_misc, fox_b_f, lqk, subln_w, w_branch, w_out,
           tab_dsa, tab_diff, tab_dil):
    h = _rmsnorm(x2, norm_w, BF16)
    pack = _matmul(h, w_pack, BF16, 1024, 512, name="proj_pack").reshape(bsz, seq, N_PACK)
    zg = _matmul(h, w_zg, F32, 1024, 512, name="proj_zg")
    misc = _matmul(h, w_misc, F32, 1024, N_MISC, name="proj_misc").reshape(bsz, seq, N_MISC)

    k2 = misc[:, :, :MISC_W].astype(BF16)
    w_idx = misc[:, :, MISC_W:MISC_F]
    o_dsa = _dsa_attention(pack, w_idx, k2, tab_dsa)

    f_bhs = jnp.transpose(misc[:, :, MISC_F:MISC_F + N_HEADS], (0, 2, 1))
    cum_bhs = _fox_cum(f_bhs, fox_b_f)
    o_fox = _fox_attention(pack, jnp.transpose(cum_bhs, (0, 2, 1)), cum_bhs)

    lambda_init = 0.8 - 0.6 * math.exp(-0.3 * layer)
    o_diff = _diff_attention(pack, tab_diff, lqk, subln_w, lambda_init)
    o_dil = _dil_attention(pack, tab_dil)

    m = bsz * seq
    branches = [o.reshape(m, BRANCH_WIDTH) for o in (o_dsa, o_fox, o_diff, o_dil)]
    merged = _merge(branches, zg, w_branch)
    return _matmul(merged, w_out, F32, 1024, 512, residual=x2, name="out_proj")


def kernel(x, norm_w, w_in, fox_b_f, diff_lq1, diff_lk1, diff_lq2, diff_lk2, diff_subln_w, w_branch, w_out,
           rel_bias, final_norm_w):
    bsz, seq, d = x.shape
    w_pack, w_zg, w_misc = _pack_w_in(w_in)
    w_branch_bf = w_branch.astype(BF16)
    w_out_bf = w_out.astype(BF16)
    lqk = jnp.stack([diff_lq1, diff_lk1, diff_lq2, diff_lk2], axis=1)
    bias_dsa, bias_diff, bias_dil = jnp.split(rel_bias, 3, axis=-1)
    tab_dsa = _bias_tiles(bias_dsa, DSA_FIRST, DSA_TILES, causal=False)
    tab_diff = _bias_tiles(bias_diff, DIFF_FIRST, DIFF_TILES, causal=True)
    tab_dil = _dilated_tiles(bias_dil, DIL_FIRST, DIL_TILES)

    x2 = x.reshape(bsz * seq, d)
    for layer in range(DEPTH):
        x2 = _layer(x2, bsz, seq, layer, norm_w[layer], w_pack[layer], w_zg[layer], w_misc[layer],
                    fox_b_f[layer], lqk[layer], diff_subln_w[layer], w_branch_bf[layer], w_out_bf[layer],
                    tab_dsa, tab_diff, tab_dil)
    return _rmsnorm(x2, final_norm_w, F32).reshape(bsz, seq, d)
```

```python
import functools
import math

import jax
import jax.numpy as jnp
import numpy as np
from jax import lax
from jax.experimental import pallas as pl
from jax.experimental.pallas import tpu as pltpu

F32 = jnp.float32
BF16 = jnp.bfloat16

D_MODEL = 2048
DEPTH = 4
HEAD_DIM = 128
N_HEADS = 4
BRANCH_WIDTH = N_HEADS * HEAD_DIM
N_BRANCHES = 4
DIFF_DIM = HEAD_DIM // 2
IDX_HEADS = 16
IDX_DIM = 64
TOPK_MAX = 256
DILATED_CONFIGS = ((128, 1), (512, 4), (2048, 16))
N_BUCKETS = 32
MAX_DISTANCE = 2048
RMS_EPS = 1e-6

LANES = 128
SUBLANES = 8
TQ = 256
TK = 256
TKF = 512
LOG2E = math.log2(math.e)
NEG = -1e30
VMEM_LIMIT = 56 * 1024 * 1024

COL_IDXQ = 0
COL_QK = IDX_HEADS * IDX_DIM
N_PACK = COL_QK + N_BRANCHES * 2 * BRANCH_WIDTH
N_V = N_BRANCHES * BRANCH_WIDTH
N_ZG = N_BRANCHES * BRANCH_WIDTH + N_BRANCHES * D_MODEL
N_MISC = 256
MISC_W = 2 * IDX_DIM
MISC_F = MISC_W + IDX_HEADS
MIX_A, MIX_B, MIX_C, MIX_D = range(N_BRANCHES)

_NT = (((1,), (1,)), ((), ()))


def _cparams(sem):
    return pltpu.CompilerParams(dimension_semantics=sem, vmem_limit_bytes=VMEM_LIMIT)


def _q_block(mixer):
    return COL_QK // BRANCH_WIDTH + 2 * mixer


def _rmsnorm_kernel(x_ref, w_ref, o_ref):
    x = x_ref[...]
    y = x * lax.rsqrt(jnp.mean(x * x, axis=-1, keepdims=True) + RMS_EPS)
    o_ref[...] = (y * w_ref[...]).astype(o_ref.dtype)


def _rmsnorm(x2, w, out_dtype, tm=512):
    m, d = x2.shape
    return pl.pallas_call(
        _rmsnorm_kernel,
        grid=(m // tm,),
        in_specs=[pl.BlockSpec((tm, d), lambda i: (i, 0)), pl.BlockSpec((1, d), lambda i: (0, 0))],
        out_specs=pl.BlockSpec((tm, d), lambda i: (i, 0)),
        out_shape=jax.ShapeDtypeStruct((m, d), out_dtype),
        compiler_params=_cparams(("parallel",)),
        name="rmsnorm",
    )(x2, w.reshape(1, d))


def _matmul_kernel(a_ref, b_ref, o_ref):
    o_ref[...] = jnp.dot(a_ref[...], b_ref[...], preferred_element_type=F32).astype(o_ref.dtype)


def _matmul_t_kernel(a_ref, b_ref, o_ref):
    o_ref[...] = jnp.dot(a_ref[...], b_ref[...], preferred_element_type=F32).T.astype(o_ref.dtype)


def _matmul_res_kernel(a_ref, b_ref, r_ref, o_ref):
    o_ref[...] = r_ref[...] + jnp.dot(a_ref[...], b_ref[...], preferred_element_type=F32)


def _matmul(a, b, out_dtype, tm, tn, residual=None, transpose_out=False, name="matmul"):
    m, k = a.shape
    n = b.shape[1]
    tm, tn = min(tm, m), min(tn, n)
    in_specs = [pl.BlockSpec((tm, k), lambda i, j: (i, 0)), pl.BlockSpec((k, tn), lambda i, j: (0, j))]
    args = [a, b]
    kern = _matmul_kernel
    out_spec = pl.BlockSpec((tm, tn), lambda i, j: (i, j))
    out_shape = (m, n)
    if residual is not None:
        in_specs.append(pl.BlockSpec((tm, tn), lambda i, j: (i, j)))
        args.append(residual)
        kern = _matmul_res_kernel
    if transpose_out:
        kern = _matmul_t_kernel
        out_spec = pl.BlockSpec((tn, tm), lambda i, j: (j, i))
        out_shape = (n, m)
    return pl.pallas_call(
        kern,
        grid=(m // tm, n // tn),
        in_specs=in_specs,
        out_specs=out_spec,
        out_shape=jax.ShapeDtypeStruct(out_shape, out_dtype),
        compiler_params=_cparams(("parallel", "arbitrary")),
        name=name,
    )(*args)


def _t5_bucket(dist):
    dist = jnp.maximum(dist, 0)
    max_exact = N_BUCKETS // 2
    d = jnp.maximum(dist, max_exact).astype(F32)
    large = max_exact + (jnp.log(d / max_exact) / math.log(MAX_DISTANCE / max_exact)
                         * (N_BUCKETS - max_exact)).astype(jnp.int32)
    large = jnp.minimum(large, N_BUCKETS - 1)
    return jnp.where(dist < max_exact, dist, large)


def _toeplitz_tiles(vals, n_tiles):
    nh, length = vals.shape
    hank = jnp.tile(vals, (1, LANES + 1))[:, :LANES * (length + 1)].reshape(nh, LANES, length + 1)
    hank = hank[:, ::-1, :n_tiles * LANES]
    return jnp.transpose(hank.reshape(nh, LANES, n_tiles, LANES), (0, 2, 1, 3))


def _distances(first, n_tiles):
    return first * LANES - (LANES - 1) + jnp.arange(n_tiles * LANES + LANES - 1)


def _bias_tiles(table, first, n_tiles, causal):
    d = _distances(first, n_tiles)
    vals = table[_t5_bucket(d)].T
    if causal:
        vals = jnp.where(d[None] >= 0, vals, NEG)
    return _toeplitz_tiles(vals * LOG2E, n_tiles)


def _dilated_tiles(table, first, n_tiles):
    d = _distances(first, n_tiles)
    mult = jnp.zeros(d.shape, F32)
    for window, dilation in DILATED_CONFIGS:
        mult = mult + ((d >= 0) & (d <= window) & (d % dilation == 0)).astype(F32)
    vals = table[_t5_bucket(d)].T + jnp.log(jnp.maximum(mult, 1.0))[None]
    return _toeplitz_tiles(jnp.where(mult[None] > 0, vals, NEG) * LOG2E, n_tiles)


def _bias_block(tab_ref, h, a, first, n_tiles, tk=TK):
    def tile(off):
        return tab_ref[h, jnp.clip(off - first, 0, n_tiles - 1)]
    rows = [jnp.concatenate([tile(a + ib - jb) for ib in range(TQ // LANES)], axis=1)
            for jb in range(tk // LANES)]
    return jnp.concatenate(rows, axis=0)


def _softmax_step(s, vt, m_ref, l_ref, acc_ref, idx):
    m_prev = m_ref[idx]
    m_next = jnp.maximum(m_prev, jnp.max(s, axis=0, keepdims=True))
    alpha = jnp.exp2(m_prev - m_next)
    p = jnp.exp2(s - m_next)
    l_ref[idx] = alpha * l_ref[idx] + jnp.sum(p, axis=0, keepdims=True)
    m_ref[idx] = m_next
    acc_ref[idx] = acc_ref[idx] * alpha + jnp.dot(vt, p.astype(BF16), preferred_element_type=F32)


def _pipelined(n, logits_fn, update_fn, lookahead=3):
    pending = {i: logits_fn(i) for i in range(min(lookahead, n))}
    for i in range(n):
        if i + lookahead < n:
            pending[i + lookahead] = logits_fn(i + lookahead)
        update_fn(i, pending.pop(i))


def _init_softmax(m_ref, l_ref, acc_ref):
    m_ref[...] = jnp.full(m_ref.shape, NEG, F32)
    l_ref[...] = jnp.zeros(l_ref.shape, F32)
    acc_ref[...] = jnp.zeros(acc_ref.shape, F32)


def _softmax_scratch(n):
    return [pltpu.VMEM((n, 1, TQ), F32), pltpu.VMEM((n, 1, TQ), F32), pltpu.VMEM((n, HEAD_DIM, TQ), F32)]


def _head(ref, h):
    return ref[0, :, h * HEAD_DIM:(h + 1) * HEAD_DIM]


def _store_head(o_ref, h, o_t):
    o_ref[0, :, h * HEAD_DIM:(h + 1) * HEAD_DIM] = o_t.T


def _logsig_cumsum_kernel(f_ref, b_ref, o_ref):
    x = f_ref[0, 0] + b_ref[0, 0]
    ls = jnp.minimum(x, 0.0) - jnp.log1p(jnp.exp(-jnp.abs(x)))
    rows = ls.shape[0]
    r = lax.broadcasted_iota(jnp.int32, (LANES, LANES), 0)
    c = lax.broadcasted_iota(jnp.int32, (LANES, LANES), 1)
    upper = (r <= c).astype(F32)
    within = jnp.dot(ls, upper, preferred_element_type=F32, precision=lax.Precision.HIGHEST)
    total = jnp.broadcast_to(within[:, LANES - 1:LANES], (rows, LANES))
    rr = lax.broadcasted_iota(jnp.int32, (rows, rows), 0)
    rc = lax.broadcasted_iota(jnp.int32, (rows, rows), 1)
    strict = (rc < rr).astype(F32)
    before = jnp.dot(strict, total, preferred_element_type=F32, precision=lax.Precision.HIGHEST)
    o_ref[0, 0] = within + before


def _fox_cum(f_bhs, b_f):
    bsz, nh, seq = f_bhs.shape
    rows = seq // LANES
    out = pl.pallas_call(
        _logsig_cumsum_kernel,
        grid=(bsz, nh),
        in_specs=[pl.BlockSpec((1, 1, rows, LANES), lambda b, h: (b, h, 0, 0)),
                  pl.BlockSpec((1, 1, 1, LANES), lambda b, h: (0, h, 0, 0))],
        out_specs=pl.BlockSpec((1, 1, rows, LANES), lambda b, h: (b, h, 0, 0)),
        out_shape=jax.ShapeDtypeStruct((bsz, nh, rows, LANES), F32),
        compiler_params=_cparams(("parallel", "parallel")),
        name="fox_cumsum",
    )(f_bhs.reshape(bsz, nh, rows, LANES), jnp.broadcast_to(b_f[None, :, None, None], (1, nh, 1, LANES)))
    return out.reshape(bsz, nh, seq)


def _last_key_tile(qi, tk):
    return (qi * TQ + TQ - 1) // tk


def _fox_kernel(q_ref, k_ref, vt_ref, cq_ref, ck_ref, o_ref, m_ref, l_ref, acc_ref):
    qi, ki = pl.program_id(1), pl.program_id(2)
    last = _last_key_tile(qi, TKF)

    @pl.when(ki == 0)
    def _():
        _init_softmax(m_ref, l_ref, acc_ref)

    def step(diagonal):
        if diagonal:
            key = ki * TKF + lax.broadcasted_iota(jnp.int32, (TKF, TQ), 0)
            qry = qi * TQ + lax.broadcasted_iota(jnp.int32, (TKF, TQ), 1)
            causal = key <= qry

        def logits(h):
            return lax.dot_general(_head(k_ref, h), _head(q_ref, h), _NT, preferred_element_type=F32)

        def update(h, qk):
            cq = cq_ref[0, h:h + 1, :] * LOG2E
            ck = ck_ref[0, :, h:h + 1] * LOG2E
            s = qk * (HEAD_DIM ** -0.5 * LOG2E) + cq - ck
            if diagonal:
                s = jnp.where(causal, s, NEG)
            _softmax_step(s, vt_ref[h * HEAD_DIM:(h + 1) * HEAD_DIM, :], m_ref, l_ref, acc_ref, h)

        _pipelined(N_HEADS, logits, update)

    @pl.when(ki < last)
    def _():
        step(False)

    @pl.when(ki == last)
    def _():
        step(True)
        for h in range(N_HEADS):
            _store_head(o_ref, h, acc_ref[h] / l_ref[h])


def _causal_specs(mixer, seq):
    nk = seq // TKF
    qb = _q_block(mixer)

    def key_tile(i, j):
        return jnp.minimum(_last_key_tile(i, TKF), j)

    return [pl.BlockSpec((1, TQ, BRANCH_WIDTH), lambda b, i, j: (b, i, qb)),
            pl.BlockSpec((1, TKF, BRANCH_WIDTH), lambda b, i, j: (b, key_tile(i, j), qb + 1)),
            pl.BlockSpec((BRANCH_WIDTH, TKF), lambda b, i, j: (mixer, b * nk + key_tile(i, j)))], key_tile


def _fox_attention(pack, vt, cum_bsh, cum_bhs):
    bsz, seq, _ = pack.shape
    specs, key_tile = _causal_specs(MIX_B, seq)
    return pl.pallas_call(
        _fox_kernel,
        grid=(bsz, seq // TQ, seq // TKF),
        in_specs=specs
        + [pl.BlockSpec((1, N_HEADS, TQ), lambda b, i, j: (b, 0, i)),
           pl.BlockSpec((1, TKF, N_HEADS), lambda b, i, j: (b, key_tile(i, j), 0))],
        out_specs=pl.BlockSpec((1, TQ, BRANCH_WIDTH), lambda b, i, j: (b, i, 0)),
        out_shape=jax.ShapeDtypeStruct((bsz, seq, BRANCH_WIDTH), F32),
        scratch_shapes=_softmax_scratch(N_HEADS),
        compiler_params=_cparams(("parallel", "parallel", "arbitrary")),
        name="fox_attention",
    )(pack, pack, vt, cum_bhs, cum_bsh)


DIFF_FIRST, DIFF_TILES = -1, 15


def _diff_kernel(lambda_init, q_ref, k_ref, vt_ref, tab_ref, lqk_ref, sub_ref, o_ref, m_ref, l_ref, acc_ref):
    qi, ki = pl.program_id(1), pl.program_id(2)
    last = _last_key_tile(qi, TKF)

    @pl.when(ki == 0)
    def _():
        _init_softmax(m_ref, l_ref, acc_ref)

    @pl.when(ki <= last)
    def _():
        lane = lax.broadcasted_iota(jnp.int32, (TQ, HEAD_DIM), 1)

        def logits(i):
            h, c = divmod(i, 2)
            q = _head(q_ref, h)
            keep = lane < DIFF_DIM if c == 0 else lane >= DIFF_DIM
            qc = jnp.where(keep, q, jnp.zeros_like(q))
            return lax.dot_general(_head(k_ref, h), qc, _NT, preferred_element_type=F32)

        def update(i, qk):
            h = i // 2
            bias = _bias_block(tab_ref, h, (qi * TQ - ki * TKF) // LANES, DIFF_FIRST, DIFF_TILES, TKF)
            _softmax_step(qk * (DIFF_DIM ** -0.5 * LOG2E) + bias, vt_ref[h * HEAD_DIM:(h + 1) * HEAD_DIM, :],
                          m_ref, l_ref, acc_ref, i)

        _pipelined(2 * N_HEADS, logits, update)

    @pl.when(ki == last)
    def _():
        lqk = lqk_ref[...]
        lam = (jnp.exp(jnp.sum(lqk[0:1] * lqk[1:2], axis=1, keepdims=True))
               - jnp.exp(jnp.sum(lqk[2:3] * lqk[3:4], axis=1, keepdims=True)) + lambda_init)
        for h in range(N_HEADS):
            o = acc_ref[2 * h] / l_ref[2 * h] - lam * (acc_ref[2 * h + 1] / l_ref[2 * h + 1])
            y = o * lax.rsqrt(jnp.mean(o * o, axis=0, keepdims=True) + RMS_EPS) * sub_ref[...]
            _store_head(o_ref, h, y * (1.0 - lambda_init))


def _diff_attention(pack, vt, tab, lqk, subln_w, lambda_init):
    bsz, seq, _ = pack.shape
    sub = jnp.broadcast_to(subln_w[:, None], (HEAD_DIM, TQ))
    specs, _ = _causal_specs(MIX_C, seq)
    return pl.pallas_call(
        functools.partial(_diff_kernel, lambda_init),
        grid=(bsz, seq // TQ, seq // TKF),
        in_specs=specs
        + [pl.BlockSpec(tab.shape, lambda b, i, j: (0, 0, 0, 0)),
           pl.BlockSpec(lqk.shape, lambda b, i, j: (0, 0)),
           pl.BlockSpec((HEAD_DIM, TQ), lambda b, i, j: (0, 0))],
        out_specs=pl.BlockSpec((1, TQ, BRANCH_WIDTH), lambda b, i, j: (b, i, 0)),
        out_shape=jax.ShapeDtypeStruct((bsz, seq, BRANCH_WIDTH), F32),
        scratch_shapes=_softmax_scratch(2 * N_HEADS),
        compiler_params=_cparams(("parallel", "parallel", "arbitrary")),
        name="diff_attention",
    )(pack, pack, vt, tab, lqk, sub)


DIL_SPAN = MAX_DISTANCE // TK + 1
DIL_FIRST = -1
DIL_TILES = (DIL_SPAN - 1) * (TK // LANES) + TQ // LANES + 1


def _dil_kernel(q_ref, k_ref, vt_ref, tab_ref, o_ref, m_ref, l_ref, acc_ref):
    qi, r = pl.program_id(1), pl.program_id(2)
    back = DIL_SPAN - 1 - r

    @pl.when(r == 0)
    def _():
        _init_softmax(m_ref, l_ref, acc_ref)

    @pl.when(qi - back >= 0)
    def _():
        def logits(h):
            return lax.dot_general(_head(k_ref, h), _head(q_ref, h), _NT, preferred_element_type=F32)

        def update(h, qk):
            s = qk * (HEAD_DIM ** -0.5 * LOG2E) + _bias_block(tab_ref, h, back * (TK // LANES), DIL_FIRST, DIL_TILES)
            _softmax_step(s, vt_ref[h * HEAD_DIM:(h + 1) * HEAD_DIM, :], m_ref, l_ref, acc_ref, h)

        _pipelined(N_HEADS, logits, update)

    @pl.when(r == DIL_SPAN - 1)
    def _():
        for h in range(N_HEADS):
            _store_head(o_ref, h, acc_ref[h] / l_ref[h])


def _dil_attention(pack, vt, tab):
    bsz, seq, _ = pack.shape
    nq = seq // TQ
    nk = seq // TK
    qb = _q_block(MIX_D)

    def key_tile(i, r):
        return jnp.maximum(i - (DIL_SPAN - 1 - r), 0)

    return pl.pallas_call(
        _dil_kernel,
        grid=(bsz, nq, DIL_SPAN),
        in_specs=[pl.BlockSpec((1, TQ, BRANCH_WIDTH), lambda b, i, r: (b, i, qb)),
                  pl.BlockSpec((1, TK, BRANCH_WIDTH), lambda b, i, r: (b, key_tile(i, r), qb + 1)),
                  pl.BlockSpec((BRANCH_WIDTH, TK), lambda b, i, r: (MIX_D, b * nk + key_tile(i, r))),
                  pl.BlockSpec(tab.shape, lambda b, i, r: (0, 0, 0, 0))],
        out_specs=pl.BlockSpec((1, TQ, BRANCH_WIDTH), lambda b, i, r: (b, i, 0)),
        out_shape=jax.ShapeDtypeStruct((bsz, seq, BRANCH_WIDTH), F32),
        scratch_shapes=_softmax_scratch(N_HEADS),
        compiler_params=_cparams(("parallel", "parallel", "arbitrary")),
        name="dilated_attention",
    )(pack, pack, vt, tab)


DSA_FIRST, DSA_TILES = -1, 15
LOWEST = -3.0e38
CHUNK = 2 * TK
ACC_ROWS = 4 * SUBLANES


def _fold(x, op):
    return op(x.reshape(CHUNK // ACC_ROWS, ACC_ROWS, TQ), axis=0)


def _dsa_kernel(topk, iq_ref, wt_ref, k2_ref, q_ref, k_ref, vt_ref, tab_ref, o_ref,
                sc_ref, thr_ref, m_ref, l_ref, acc_ref):
    qi = pl.program_id(1)
    n_tiles = qi + 1
    lane = lax.broadcasted_iota(jnp.int32, (TQ, 2 * IDX_DIM), 1)
    wt = wt_ref[0] * (IDX_HEADS ** -0.5 * IDX_DIM ** -0.5)
    key_i = lax.broadcasted_iota(jnp.int32, (TK, TQ), 0)
    qry_i = lax.broadcasted_iota(jnp.int32, (TK, TQ), 1)

    def tile_rows(kt):
        return pl.ds(pl.multiple_of(kt * TK, TK), TK)

    def score_tile(kt, diagonal):
        k2 = k2_ref[0, tile_rows(kt), :]
        acc = jnp.zeros((TK, TQ), F32)
        for pair in range(IDX_HEADS // 2):
            qp = iq_ref[0, :, pair * 2 * IDX_DIM:(pair + 1) * 2 * IDX_DIM]
            for c in range(2):
                keep = lane < IDX_DIM if c == 0 else lane >= IDX_DIM
                qc = jnp.where(keep, qp, jnp.zeros_like(qp))
                rel = jnp.maximum(lax.dot_general(k2, qc, _NT, preferred_element_type=F32), 0.0)
                hh = 2 * pair + c
                acc = acc + wt[hh:hh + 1, :] * rel
        if diagonal:
            acc = jnp.where(key_i <= qry_i, acc, -jnp.inf)
        sc_ref[tile_rows(kt), :] = acc

    def score_body(kt, carry):
        score_tile(kt, False)
        return carry

    lax.fori_loop(0, qi, score_body, 0)
    score_tile(qi, True)
    n_chunks = (n_tiles + 1) // 2

    @pl.when(n_tiles % 2 == 1)
    def _():
        sc_ref[tile_rows(n_tiles), :] = jnp.full((TK, TQ), -jnp.inf, F32)

    def chunk_rows(c):
        return pl.ds(pl.multiple_of(c * CHUNK, CHUNK), CHUNK)

    key_c = lax.broadcasted_iota(jnp.int32, (CHUNK, TQ), 0)

    def reduce_tiles(fn, init, reduce, combine):
        def body(c, carry):
            return combine(carry, _fold(fn(sc_ref[chunk_rows(c), :], c), reduce))
        return reduce(lax.fori_loop(0, n_chunks, body, init), axis=0, keepdims=True)

    def count(pred):
        return reduce_tiles(lambda blk, kt: jnp.where(pred(blk, kt), 1.0, 0.0),
                            jnp.zeros((ACC_ROWS, TQ), F32), jnp.sum, jnp.add)

    t_q = qi * TQ + lax.broadcasted_iota(jnp.int32, (1, TQ), 1)
    few = t_q < topk
    row_max = reduce_tiles(lambda blk, kt: blk, jnp.full((ACC_ROWS, TQ), -jnp.inf, F32), jnp.max, jnp.maximum)
    row_min = reduce_tiles(lambda blk, kt: jnp.where(blk == -jnp.inf, jnp.inf, blk),
                           jnp.full((ACC_ROWS, TQ), jnp.inf, F32), jnp.min, jnp.minimum)
    kf = float(topk)

    def cond(st):
        return jnp.logical_and(st[0] < 400, jnp.min(st[4]) < 0.5)

    def body(st):
        it, lo, hi, thr, done, tie = st
        mid = lo + (hi - lo) * 0.5
        cnt = count(lambda blk, kt: blk >= mid)
        live = done < 0.5
        exact = jnp.logical_and(live, cnt == kf)
        stuck = jnp.logical_and(jnp.logical_and(live, cnt != kf), jnp.logical_or(mid <= lo, mid >= hi))
        thr = jnp.where(exact, mid, jnp.where(stuck, lo, thr))
        tie = jnp.where(stuck, 1.0, tie)
        done = jnp.where(jnp.logical_or(exact, stuck), 1.0, done)
        lo = jnp.where(cnt > kf, mid, lo)
        hi = jnp.where(cnt < kf, mid, hi)
        return it + 1, lo, hi, thr, done, tie

    init = (jnp.int32(0), row_min, row_max + (jnp.abs(row_max) * 1e-6 + 1e-30),
            jnp.full((1, TQ), LOWEST, F32), jnp.where(few, 1.0, 0.0), jnp.zeros((1, TQ), F32))
    _, _, _, thr, _, tie = lax.while_loop(cond, body, init)
    thr_ref[...] = jnp.broadcast_to(thr, thr_ref.shape)

    @pl.when(jnp.max(tie) > 0.5)
    def _():
        is_tie = tie > 0.5
        need = kf - count(lambda blk, kt: blk > thr)

        def kept(limit):
            return count(lambda blk, c: jnp.logical_and(blk == thr, c * CHUNK + key_c <= limit))

        def jbody(_, st):
            jlo, jhi = st
            jmid = (jlo + jhi) // 2
            ok = kept(jmid) >= need
            return jnp.where(ok, jlo, jmid), jnp.where(ok, jmid, jhi)

        jlo0 = jnp.full((1, TQ), -1, jnp.int32)
        jhi0 = jnp.zeros((1, TQ), jnp.int32) + (n_chunks * CHUNK - 1)
        _, cut = lax.fori_loop(0, 14, jbody, (jlo0, jhi0))

        def drop(c, carry):
            blk = sc_ref[chunk_rows(c), :]
            extra = jnp.logical_and(jnp.logical_and(is_tie, blk == thr), c * CHUNK + key_c > cut)
            sc_ref[chunk_rows(c), :] = jnp.where(extra, -jnp.inf, blk)
            return carry

        lax.fori_loop(0, n_chunks, drop, 0)

    _init_softmax(m_ref, l_ref, acc_ref)

    def attend(c, carry):
        tiles = [2 * c, 2 * c + 1]
        sel = [sc_ref[tile_rows(kt), :] >= thr_ref[0:1, :] for kt in tiles]

        def logits(i):
            t, h = divmod(i, N_HEADS)
            hs = slice(h * HEAD_DIM, (h + 1) * HEAD_DIM)
            return lax.dot_general(k_ref[0, tile_rows(tiles[t]), hs], _head(q_ref, h), _NT,
                                   preferred_element_type=F32)

        def update(i, qk):
            t, h = divmod(i, N_HEADS)
            hs = slice(h * HEAD_DIM, (h + 1) * HEAD_DIM)
            s = qk * (HEAD_DIM ** -0.5 * LOG2E) + _bias_block(tab_ref, h, (qi - tiles[t]) * (TK // LANES),
                                                              DSA_FIRST, DSA_TILES)
            _softmax_step(jnp.where(sel[t], s, NEG), vt_ref[hs, tile_rows(tiles[t])], m_ref, l_ref, acc_ref, h)

        _pipelined(2 * N_HEADS, logits, update)
        return carry

    lax.fori_loop(0, n_chunks, attend, 0)
    for h in range(N_HEADS):
        _store_head(o_ref, h, acc_ref[h] / l_ref[h])


def _dsa_attention(pack, vt, w_t, k2, tab):
    bsz, seq, _ = pack.shape
    nq = seq // TQ
    topk = min(TOPK_MAX, seq // 4)
    qb = _q_block(MIX_A)
    return pl.pallas_call(
        functools.partial(_dsa_kernel, topk),
        grid=(bsz, nq),
        in_specs=[pl.BlockSpec((1, TQ, IDX_HEADS * IDX_DIM), lambda b, i: (b, i, 0)),
                  pl.BlockSpec((1, IDX_HEADS, TQ), lambda b, i: (b, 0, i)),
                  pl.BlockSpec((1, seq, 2 * IDX_DIM), lambda b, i: (b, 0, 0)),
                  pl.BlockSpec((1, TQ, BRANCH_WIDTH), lambda b, i: (b, i, qb)),
                  pl.BlockSpec((1, seq, BRANCH_WIDTH), lambda b, i: (b, 0, qb + 1)),
                  pl.BlockSpec((BRANCH_WIDTH, seq), lambda b, i: (MIX_A, b)),
                  pl.BlockSpec(tab.shape, lambda b, i: (0, 0, 0, 0))],
        out_specs=pl.BlockSpec((1, TQ, BRANCH_WIDTH), lambda b, i: (b, i, 0)),
        out_shape=jax.ShapeDtypeStruct((bsz, seq, BRANCH_WIDTH), F32),
        scratch_shapes=[pltpu.VMEM((seq, TQ), F32), pltpu.VMEM((SUBLANES, TQ), F32)] + _softmax_scratch(N_HEADS),
        compiler_params=_cparams(("parallel", "arbitrary")),
        name="dsa_attention",
    )(pack, w_t, k2, pack, pack, vt, tab)


def _merge_kernel(oa_ref, ob_ref, oc_ref, od_ref, z_ref, g0_ref, g1_ref, g2_ref, g3_ref, wb_ref, o_ref):
    merged = None
    for b, (o_ref_b, g_ref) in enumerate(zip((oa_ref, ob_ref, oc_ref, od_ref),
                                             (g0_ref, g1_ref, g2_ref, g3_ref))):
        z = z_ref[:, b * BRANCH_WIDTH:(b + 1) * BRANCH_WIDTH]
        y = o_ref_b[...] * (z * jax.nn.sigmoid(z))
        term = jax.nn.sigmoid(g_ref[...]) * jnp.dot(y.astype(BF16), wb_ref[b], preferred_element_type=F32)
        merged = term if merged is None else merged + term
    o_ref[...] = merged.astype(o_ref.dtype)


def _merge(branches, zg, w_branch, tm=256):
    m = zg.shape[0]
    o_spec = pl.BlockSpec((tm, BRANCH_WIDTH), lambda i: (i, 0))
    g_specs = [pl.BlockSpec((tm, D_MODEL), functools.partial(lambda i, b: (i, 1 + b), b=b))
               for b in range(N_BRANCHES)]
    return pl.pallas_call(
        _merge_kernel,
        grid=(m // tm,),
        in_specs=[o_spec] * N_BRANCHES + [pl.BlockSpec((tm, N_BRANCHES * BRANCH_WIDTH), lambda i: (i, 0))]
        + g_specs + [pl.BlockSpec(w_branch.shape, lambda i: (0, 0, 0))],
        out_specs=pl.BlockSpec((tm, D_MODEL), lambda i: (i, 0)),
        out_shape=jax.ShapeDtypeStruct((m, D_MODEL), BF16),
        compiler_params=_cparams(("parallel",)),
        name="gated_merge",
    )(*branches, zg, zg, zg, zg, zg, w_branch)


def _pack_w_in(w_in):
    sizes = (N_BRANCHES * BRANCH_WIDTH, N_BRANCHES * D_MODEL, 3 * BRANCH_WIDTH, IDX_HEADS * IDX_DIM,
             IDX_DIM, IDX_HEADS, 3 * BRANCH_WIDTH, N_HEADS, 3 * BRANCH_WIDTH, 3 * BRANCH_WIDTH)
    offs = np.concatenate([[0], np.cumsum(sizes)])
    z, gate, a, iq, ik, iw, b, ff, c, d = [w_in[:, :, offs[n]:offs[n + 1]] for n in range(len(sizes))]
    qk = [m[:, :, :2 * BRANCH_WIDTH] for m in (a, b, c, d)]
    v = [m[:, :, 2 * BRANCH_WIDTH:] for m in (a, b, c, d)]
    pad = jnp.zeros(w_in.shape[:2] + (N_MISC - MISC_F - N_HEADS,), w_in.dtype)
    return (jnp.concatenate([iq] + qk, axis=-1).astype(BF16),
            jnp.concatenate(v, axis=-1).astype(BF16),
            jnp.concatenate([z, gate], axis=-1).astype(BF16),
            jnp.concatenate([ik, ik, iw, ff, pad], axis=-1).astype(BF16))


def _layer(x2, bsz, seq, layer, norm_w, w_pack, w_v, w_zg, w_misc, fox_b_f, lqk, subln_w, w_branch, w_out,
           tab_dsa, tab_diff, tab_dil):
    h = _rmsnorm(x2, norm_w, BF16)
    pack = _matmul(h, w_pack, BF16, 1024, 512, name="proj_pack").reshape(bsz, seq, N_PACK)
    vt = _matmul(h, w_v, BF16, 1024, 512, transpose_out=True, name="proj_vt")
    zg = _matmul(h, w_zg, F32, 1024, 512, name="proj_zg")
    misc = _matmul(h, w_misc, F32, 1024, N_MISC, name="proj_misc").reshape(bsz, seq, N_MISC)

    k2 = misc[:, :, :MISC_W].astype(BF16)
    w_t = jnp.transpose(misc[:, :, MISC_W:MISC_F], (0, 2, 1))
    o_dsa = _dsa_attention(pack, vt, w_t, k2, tab_dsa)

    f_bsh = misc[:, :, MISC_F:MISC_F + N_HEADS]
    cum_bhs = _fox_cum(jnp.transpose(f_bsh, (0, 2, 1)), fox_b_f)
    o_fox = _fox_attention(pack, vt, jnp.transpose(cum_bhs, (0, 2, 1)), cum_bhs)

    lambda_init = 0.8 - 0.6 * math.exp(-0.3 * layer)
    o_diff = _diff_attention(pack, vt, tab_diff, lqk, subln_w, lambda_init)
    o_dil = _dil_attention(pack, vt, tab_dil)

    m = bsz * seq
    branches = [o.reshape(m, BRANCH_WIDTH) for o in (o_dsa, o_fox, o_diff, o_dil)]
    merged = _merge(branches, zg, w_branch)
    return _matmul(merged, w_out, F32, 1024, 512, residual=x2, name="out_proj")


def kernel(x, norm_w, w_in, fox_b_f, diff_lq1, diff_lk1, diff_lq2, diff_lk2, diff_subln_w, w_branch, w_out,
           rel_bias, final_norm_w):
    bsz, seq, d = x.shape
    w_pack, w_v, w_zg, w_misc = _pack_w_in(w_in)
    w_branch_bf = w_branch.astype(BF16)
    w_out_bf = w_out.astype(BF16)
    lqk = jnp.stack([diff_lq1, diff_lk1, diff_lq2, diff_lk2], axis=1)
    bias_dsa, bias_diff, bias_dil = jnp.split(rel_bias, 3, axis=-1)
    tab_dsa = _bias_tiles(bias_dsa, DSA_FIRST, DSA_TILES, causal=False)
    tab_diff = _bias_tiles(bias_diff, DIFF_FIRST, DIFF_TILES, causal=True)
    tab_dil = _dilated_tiles(bias_dil, DIL_FIRST, DIL_TILES)

    x2 = x.reshape(bsz * seq, d)
    for layer in range(DEPTH):
        x2 = _layer(x2, bsz, seq, layer, norm_w[layer], w_pack[layer], w_v[layer], w_zg[layer], w_misc[layer],
                    fox_b_f[layer], lqk[layer], diff_subln_w[layer], w_branch_bf[layer], w_out_bf[layer],
                    tab_dsa, tab_diff, tab_dil)
    return _rmsnorm(x2, final_norm_w, F32).reshape(bsz, seq, d)
```

```python
import functools
import math

import jax
import jax.numpy as jnp
import numpy as np
from jax import lax
from jax.experimental import pallas as pl
from jax.experimental.pallas import tpu as pltpu

F32 = jnp.float32
BF16 = jnp.bfloat16

D_MODEL = 2048
DEPTH = 4
HEAD_DIM = 128
N_HEADS = 4
BRANCH_WIDTH = N_HEADS * HEAD_DIM
N_BRANCHES = 4
DIFF_DIM = HEAD_DIM // 2
IDX_HEADS = 16
IDX_DIM = 64
TOPK_MAX = 256
DILATED_CONFIGS = ((128, 1), (512, 4), (2048, 16))
N_BUCKETS = 32
MAX_DISTANCE = 2048
RMS_EPS = 1e-6

LANES = 128
SUBLANES = 8
TQ = 256
TK = 256
TKF = 512
LOG2E = math.log2(math.e)
NEG = -1e30
VMEM_LIMIT = 56 * 1024 * 1024

COL_IDXQ = 0
COL_QK = IDX_HEADS * IDX_DIM
N_PACK = COL_QK + N_BRANCHES * 2 * BRANCH_WIDTH
N_V = N_BRANCHES * BRANCH_WIDTH
N_ZG = N_BRANCHES * BRANCH_WIDTH + N_BRANCHES * D_MODEL
N_MISC = 256
MISC_W = 2 * IDX_DIM
MISC_F = MISC_W + IDX_HEADS
MIX_A, MIX_B, MIX_C, MIX_D = range(N_BRANCHES)

_NT = (((1,), (1,)), ((), ()))


def _cparams(sem):
    return pltpu.CompilerParams(dimension_semantics=sem, vmem_limit_bytes=VMEM_LIMIT)


def _q_block(mixer):
    return COL_QK // BRANCH_WIDTH + 2 * mixer


def _rmsnorm_kernel(x_ref, w_ref, o_ref):
    x = x_ref[...]
    y = x * lax.rsqrt(jnp.mean(x * x, axis=-1, keepdims=True) + RMS_EPS)
    o_ref[...] = (y * w_ref[...]).astype(o_ref.dtype)


def _rmsnorm(x2, w, out_dtype, tm=512):
    m, d = x2.shape
    return pl.pallas_call(
        _rmsnorm_kernel,
        grid=(m // tm,),
        in_specs=[pl.BlockSpec((tm, d), lambda i: (i, 0)), pl.BlockSpec((1, d), lambda i: (0, 0))],
        out_specs=pl.BlockSpec((tm, d), lambda i: (i, 0)),
        out_shape=jax.ShapeDtypeStruct((m, d), out_dtype),
        compiler_params=_cparams(("parallel",)),
        name="rmsnorm",
    )(x2, w.reshape(1, d))


def _matmul_kernel(a_ref, b_ref, o_ref):
    o_ref[...] = jnp.dot(a_ref[...], b_ref[...], preferred_element_type=F32).astype(o_ref.dtype)


def _matmul_t_kernel(a_ref, b_ref, o_ref):
    o_ref[...] = jnp.dot(a_ref[...], b_ref[...], preferred_element_type=F32).T.astype(o_ref.dtype)


def _matmul_res_kernel(a_ref, b_ref, r_ref, o_ref):
    o_ref[...] = r_ref[...] + jnp.dot(a_ref[...], b_ref[...], preferred_element_type=F32)


def _matmul(a, b, out_dtype, tm, tn, residual=None, transpose_out=False, name="matmul"):
    m, k = a.shape
    n = b.shape[1]
    tm, tn = min(tm, m), min(tn, n)
    in_specs = [pl.BlockSpec((tm, k), lambda i, j: (i, 0)), pl.BlockSpec((k, tn), lambda i, j: (0, j))]
    args = [a, b]
    kern = _matmul_kernel
    out_spec = pl.BlockSpec((tm, tn), lambda i, j: (i, j))
    out_shape = (m, n)
    if residual is not None:
        in_specs.append(pl.BlockSpec((tm, tn), lambda i, j: (i, j)))
        args.append(residual)
        kern = _matmul_res_kernel
    if transpose_out:
        kern = _matmul_t_kernel
        out_spec = pl.BlockSpec((tn, tm), lambda i, j: (j, i))
        out_shape = (n, m)
    return pl.pallas_call(
        kern,
        grid=(m // tm, n // tn),
        in_specs=in_specs,
        out_specs=out_spec,
        out_shape=jax.ShapeDtypeStruct(out_shape, out_dtype),
        compiler_params=_cparams(("parallel", "arbitrary")),
        name=name,
    )(*args)


def _t5_bucket(dist):
    dist = jnp.maximum(dist, 0)
    max_exact = N_BUCKETS // 2
    d = jnp.maximum(dist, max_exact).astype(F32)
    large = max_exact + (jnp.log(d / max_exact) / math.log(MAX_DISTANCE / max_exact)
                         * (N_BUCKETS - max_exact)).astype(jnp.int32)
    large = jnp.minimum(large, N_BUCKETS - 1)
    return jnp.where(dist < max_exact, dist, large)


def _toeplitz_tiles(vals, n_tiles):
    nh, length = vals.shape
    hank = jnp.tile(vals, (1, LANES + 1))[:, :LANES * (length + 1)].reshape(nh, LANES, length + 1)
    hank = hank[:, ::-1, :n_tiles * LANES]
    return jnp.transpose(hank.reshape(nh, LANES, n_tiles, LANES), (0, 2, 1, 3))


def _distances(first, n_tiles):
    return first * LANES - (LANES - 1) + jnp.arange(n_tiles * LANES + LANES - 1)


def _bias_tiles(table, first, n_tiles, causal):
    d = _distances(first, n_tiles)
    vals = table[_t5_bucket(d)].T
    if causal:
        vals = jnp.where(d[None] >= 0, vals, NEG)
    return _toeplitz_tiles(vals * LOG2E, n_tiles)


def _dilated_tiles(table, first, n_tiles):
    d = _distances(first, n_tiles)
    mult = jnp.zeros(d.shape, F32)
    for window, dilation in DILATED_CONFIGS:
        mult = mult + ((d >= 0) & (d <= window) & (d % dilation == 0)).astype(F32)
    vals = table[_t5_bucket(d)].T + jnp.log(jnp.maximum(mult, 1.0))[None]
    return _toeplitz_tiles(jnp.where(mult[None] > 0, vals, NEG) * LOG2E, n_tiles)


def _bias_block(tab_ref, h, a, first, n_tiles, tk=TK):
    def tile(off):
        return tab_ref[h, jnp.clip(off - first, 0, n_tiles - 1)]
    rows = [jnp.concatenate([tile(a + ib - jb) for ib in range(TQ // LANES)], axis=1)
            for jb in range(tk // LANES)]
    return jnp.concatenate(rows, axis=0)


def _softmax_step(s, vt, m_ref, l_ref, acc_ref, idx):
    m_prev = m_ref[idx]
    m_next = jnp.maximum(m_prev, jnp.max(s, axis=0, keepdims=True))
    alpha = jnp.exp2(m_prev - m_next)
    p = jnp.exp2(s - m_next)
    l_ref[idx] = alpha * l_ref[idx] + jnp.sum(p, axis=0, keepdims=True)
    m_ref[idx] = m_next
    acc_ref[idx] = acc_ref[idx] * alpha + jnp.dot(vt, p.astype(BF16), preferred_element_type=F32)


def _pipelined(n, logits_fn, update_fn, lookahead=3):
    pending = {i: logits_fn(i) for i in range(min(lookahead, n))}
    for i in range(n):
        if i + lookahead < n:
            pending[i + lookahead] = logits_fn(i + lookahead)
        update_fn(i, pending.pop(i))


def _init_softmax(m_ref, l_ref, acc_ref):
    m_ref[...] = jnp.full(m_ref.shape, NEG, F32)
    l_ref[...] = jnp.zeros(l_ref.shape, F32)
    acc_ref[...] = jnp.zeros(acc_ref.shape, F32)


def _softmax_scratch(n):
    return [pltpu.VMEM((n, 1, TQ), F32), pltpu.VMEM((n, 1, TQ), F32), pltpu.VMEM((n, HEAD_DIM, TQ), F32)]


def _head(ref, h):
    return ref[0, :, h * HEAD_DIM:(h + 1) * HEAD_DIM]


def _store_head(o_ref, h, o_t):
    o_ref[0, :, h * HEAD_DIM:(h + 1) * HEAD_DIM] = o_t.T


def _logsig_cumsum_kernel(f_ref, b_ref, o_ref):
    x = f_ref[0, 0] + b_ref[0, 0]
    ls = jnp.minimum(x, 0.0) - jnp.log1p(jnp.exp(-jnp.abs(x)))
    rows = ls.shape[0]
    r = lax.broadcasted_iota(jnp.int32, (LANES, LANES), 0)
    c = lax.broadcasted_iota(jnp.int32, (LANES, LANES), 1)
    upper = (r <= c).astype(F32)
    within = jnp.dot(ls, upper, preferred_element_type=F32, precision=lax.Precision.HIGHEST)
    total = jnp.broadcast_to(within[:, LANES - 1:LANES], (rows, LANES))
    rr = lax.broadcasted_iota(jnp.int32, (rows, rows), 0)
    rc = lax.broadcasted_iota(jnp.int32, (rows, rows), 1)
    strict = (rc < rr).astype(F32)
    before = jnp.dot(strict, total, preferred_element_type=F32, precision=lax.Precision.HIGHEST)
    o_ref[0, 0] = within + before


def _fox_cum(f_bhs, b_f):
    bsz, nh, seq = f_bhs.shape
    rows = seq // LANES
    out = pl.pallas_call(
        _logsig_cumsum_kernel,
        grid=(bsz, nh),
        in_specs=[pl.BlockSpec((1, 1, rows, LANES), lambda b, h: (b, h, 0, 0)),
                  pl.BlockSpec((1, 1, 1, LANES), lambda b, h: (0, h, 0, 0))],
        out_specs=pl.BlockSpec((1, 1, rows, LANES), lambda b, h: (b, h, 0, 0)),
        out_shape=jax.ShapeDtypeStruct((bsz, nh, rows, LANES), F32),
        compiler_params=_cparams(("parallel", "parallel")),
        name="fox_cumsum",
    )(f_bhs.reshape(bsz, nh, rows, LANES), jnp.broadcast_to(b_f[None, :, None, None], (1, nh, 1, LANES)))
    return out.reshape(bsz, nh, seq)


def _last_key_tile(qi, tk):
    return (qi * TQ + TQ - 1) // tk


def _key_rows(c, tk):
    return pl.ds(pl.multiple_of(c * tk, tk), tk)


def _head_cols(h):
    return slice(h * HEAD_DIM, (h + 1) * HEAD_DIM)


def _seq_specs(mixer, seq):
    qb = _q_block(mixer)
    return [pl.BlockSpec((1, TQ, BRANCH_WIDTH), lambda b, i: (b, i, qb)),
            pl.BlockSpec((1, seq, BRANCH_WIDTH), lambda b, i: (b, 0, qb + 1)),
            pl.BlockSpec((BRANCH_WIDTH, seq), lambda b, i: (mixer, b))]


def _fox_kernel(q_ref, k_ref, vt_ref, cq_ref, ck_ref, o_ref, m_ref, l_ref, acc_ref):
    qi = pl.program_id(1)
    last = _last_key_tile(qi, TKF)
    _init_softmax(m_ref, l_ref, acc_ref)

    def chunk(c, diagonal):
        rows = _key_rows(c, TKF)
        if diagonal:
            key = c * TKF + lax.broadcasted_iota(jnp.int32, (TKF, TQ), 0)
            qry = qi * TQ + lax.broadcasted_iota(jnp.int32, (TKF, TQ), 1)
            causal = key <= qry

        def logits(h):
            return lax.dot_general(k_ref[0, rows, _head_cols(h)], _head(q_ref, h), _NT,
                                   preferred_element_type=F32)

        def update(h, qk):
            cq = cq_ref[0, h:h + 1, :] * LOG2E
            ck = ck_ref[0, rows, h:h + 1] * LOG2E
            s = qk * (HEAD_DIM ** -0.5 * LOG2E) + cq - ck
            if diagonal:
                s = jnp.where(causal, s, NEG)
            _softmax_step(s, vt_ref[_head_cols(h), rows], m_ref, l_ref, acc_ref, h)

        _pipelined(N_HEADS, logits, update)

    def body(c, carry):
        chunk(c, False)
        return carry

    lax.fori_loop(0, last, body, 0)
    chunk(last, True)
    for h in range(N_HEADS):
        _store_head(o_ref, h, acc_ref[h] / l_ref[h])


def _fox_attention(pack, vt, cum_bsh, cum_bhs):
    bsz, seq, _ = pack.shape
    return pl.pallas_call(
        _fox_kernel,
        grid=(bsz, seq // TQ),
        in_specs=_seq_specs(MIX_B, seq)
        + [pl.BlockSpec((1, N_HEADS, TQ), lambda b, i: (b, 0, i)),
           pl.BlockSpec((1, seq, N_HEADS), lambda b, i: (b, 0, 0))],
        out_specs=pl.BlockSpec((1, TQ, BRANCH_WIDTH), lambda b, i: (b, i, 0)),
        out_shape=jax.ShapeDtypeStruct((bsz, seq, BRANCH_WIDTH), F32),
        scratch_shapes=_softmax_scratch(N_HEADS),
        compiler_params=_cparams(("parallel", "arbitrary")),
        name="fox_attention",
    )(pack, pack, vt, cum_bhs, cum_bsh)


DIFF_FIRST, DIFF_TILES = -1, 15


def _diff_kernel(lambda_init, q_ref, k_ref, vt_ref, tab_ref, lqk_ref, sub_ref, o_ref, m_ref, l_ref, acc_ref):
    qi = pl.program_id(1)
    _init_softmax(m_ref, l_ref, acc_ref)
    lane = lax.broadcasted_iota(jnp.int32, (TQ, HEAD_DIM), 1)

    def body(c, carry):
        rows = _key_rows(c, TKF)

        def logits(i):
            h, m = divmod(i, 2)
            q = _head(q_ref, h)
            keep = lane < DIFF_DIM if m == 0 else lane >= DIFF_DIM
            qc = jnp.where(keep, q, jnp.zeros_like(q))
            return lax.dot_general(k_ref[0, rows, _head_cols(h)], qc, _NT, preferred_element_type=F32)

        def update(i, qk):
            h = i // 2
            bias = _bias_block(tab_ref, h, (qi * TQ - c * TKF) // LANES, DIFF_FIRST, DIFF_TILES, TKF)
            _softmax_step(qk * (DIFF_DIM ** -0.5 * LOG2E) + bias, vt_ref[_head_cols(h), rows],
                          m_ref, l_ref, acc_ref, i)

        _pipelined(2 * N_HEADS, logits, update)
        return carry

    lax.fori_loop(0, _last_key_tile(qi, TKF) + 1, body, 0)

    lqk = lqk_ref[...]
    lam = (jnp.exp(jnp.sum(lqk[0:1] * lqk[1:2], axis=1, keepdims=True))
           - jnp.exp(jnp.sum(lqk[2:3] * lqk[3:4], axis=1, keepdims=True)) + lambda_init)
    for h in range(N_HEADS):
        o = acc_ref[2 * h] / l_ref[2 * h] - lam * (acc_ref[2 * h + 1] / l_ref[2 * h + 1])
        y = o * lax.rsqrt(jnp.mean(o * o, axis=0, keepdims=True) + RMS_EPS) * sub_ref[...]
        _store_head(o_ref, h, y * (1.0 - lambda_init))


def _diff_attention(pack, vt, tab, lqk, subln_w, lambda_init):
    bsz, seq, _ = pack.shape
    sub = jnp.broadcast_to(subln_w[:, None], (HEAD_DIM, TQ))
    return pl.pallas_call(
        functools.partial(_diff_kernel, lambda_init),
        grid=(bsz, seq // TQ),
        in_specs=_seq_specs(MIX_C, seq)
        + [pl.BlockSpec(tab.shape, lambda b, i: (0, 0, 0, 0)),
           pl.BlockSpec(lqk.shape, lambda b, i: (0, 0)),
           pl.BlockSpec((HEAD_DIM, TQ), lambda b, i: (0, 0))],
        out_specs=pl.BlockSpec((1, TQ, BRANCH_WIDTH), lambda b, i: (b, i, 0)),
        out_shape=jax.ShapeDtypeStruct((bsz, seq, BRANCH_WIDTH), F32),
        scratch_shapes=_softmax_scratch(2 * N_HEADS),
        compiler_params=_cparams(("parallel", "arbitrary")),
        name="diff_attention",
    )(pack, pack, vt, tab, lqk, sub)


DIL_FIRST = -1
DIL_TILES = MAX_DISTANCE // LANES + 1 - DIL_FIRST + 1


def _dil_kernel(q_ref, k_ref, vt_ref, tab_ref, o_ref, m_ref, l_ref, acc_ref):
    qi = pl.program_id(1)
    _init_softmax(m_ref, l_ref, acc_ref)

    def body(c, carry):
        rows = _key_rows(c, TKF)

        def logits(h):
            return lax.dot_general(k_ref[0, rows, _head_cols(h)], _head(q_ref, h), _NT,
                                   preferred_element_type=F32)

        def update(h, qk):
            bias = _bias_block(tab_ref, h, (qi * TQ - c * TKF) // LANES, DIL_FIRST, DIL_TILES, TKF)
            _softmax_step(qk * (HEAD_DIM ** -0.5 * LOG2E) + bias, vt_ref[_head_cols(h), rows],
                          m_ref, l_ref, acc_ref, h)

        _pipelined(N_HEADS, logits, update)
        return carry

    first = jnp.maximum(qi * TQ - MAX_DISTANCE, 0) // TKF
    lax.fori_loop(first, _last_key_tile(qi, TKF) + 1, body, 0)
    for h in range(N_HEADS):
        _store_head(o_ref, h, acc_ref[h] / l_ref[h])


def _dil_attention(pack, vt, tab):
    bsz, seq, _ = pack.shape
    return pl.pallas_call(
        _dil_kernel,
        grid=(bsz, seq // TQ),
        in_specs=_seq_specs(MIX_D, seq) + [pl.BlockSpec(tab.shape, lambda b, i: (0, 0, 0, 0))],
        out_specs=pl.BlockSpec((1, TQ, BRANCH_WIDTH), lambda b, i: (b, i, 0)),
        out_shape=jax.ShapeDtypeStruct((bsz, seq, BRANCH_WIDTH), F32),
        scratch_shapes=_softmax_scratch(N_HEADS),
        compiler_params=_cparams(("parallel", "arbitrary")),
        name="dilated_attention",
    )(pack, pack, vt, tab)


DSA_FIRST, DSA_TILES = -1, 15
LOWEST = -3.0e38
CHUNK = 2 * TK
ACC_ROWS = 4 * SUBLANES


def _fold(x, op):
    return op(x.reshape(CHUNK // ACC_ROWS, ACC_ROWS, TQ), axis=0)


def _dsa_kernel(topk, iq_ref, wt_ref, k2_ref, q_ref, k_ref, vt_ref, tab_ref, o_ref,
                sc_ref, thr_ref, m_ref, l_ref, acc_ref):
    qi = pl.program_id(1)
    n_tiles = qi + 1
    lane = lax.broadcasted_iota(jnp.int32, (TQ, 2 * IDX_DIM), 1)
    wt = wt_ref[0] * (IDX_HEADS ** -0.5 * IDX_DIM ** -0.5)
    key_i = lax.broadcasted_iota(jnp.int32, (TK, TQ), 0)
    qry_i = lax.broadcasted_iota(jnp.int32, (TK, TQ), 1)

    def tile_rows(kt):
        return pl.ds(pl.multiple_of(kt * TK, TK), TK)

    def score_tile(kt, diagonal):
        k2 = k2_ref[0, tile_rows(kt), :]
        acc = jnp.zeros((TK, TQ), F32)
        for pair in range(IDX_HEADS // 2):
            qp = iq_ref[0, :, pair * 2 * IDX_DIM:(pair + 1) * 2 * IDX_DIM]
            for c in range(2):
                keep = lane < IDX_DIM if c == 0 else lane >= IDX_DIM
                qc = jnp.where(keep, qp, jnp.zeros_like(qp))
                rel = jnp.maximum(lax.dot_general(k2, qc, _NT, preferred_element_type=F32), 0.0)
                hh = 2 * pair + c
                acc = acc + wt[hh:hh + 1, :] * rel
        if diagonal:
            acc = jnp.where(key_i <= qry_i, acc, -jnp.inf)
        sc_ref[tile_rows(kt), :] = acc

    def score_body(kt, carry):
        score_tile(kt, False)
        return carry

    lax.fori_loop(0, qi, score_body, 0)
    score_tile(qi, True)
    n_chunks = (n_tiles + 1) // 2

    @pl.when(n_tiles % 2 == 1)
    def _():
        sc_ref[tile_rows(n_tiles), :] = jnp.full((TK, TQ), -jnp.inf, F32)

    def chunk_rows(c):
        return pl.ds(pl.multiple_of(c * CHUNK, CHUNK), CHUNK)

    key_c = lax.broadcasted_iota(jnp.int32, (CHUNK, TQ), 0)

    def reduce_tiles(fn, init, reduce, combine):
        def body(c, carry):
            return combine(carry, _fold(fn(sc_ref[chunk_rows(c), :], c), reduce))
        return reduce(lax.fori_loop(0, n_chunks, body, init), axis=0, keepdims=True)

    def count(pred):
        return reduce_tiles(lambda blk, kt: jnp.where(pred(blk, kt), 1.0, 0.0),
                            jnp.zeros((ACC_ROWS, TQ), F32), jnp.sum, jnp.add)

    t_q = qi * TQ + lax.broadcasted_iota(jnp.int32, (1, TQ), 1)
    few = t_q < topk
    row_max = reduce_tiles(lambda blk, kt: blk, jnp.full((ACC_ROWS, TQ), -jnp.inf, F32), jnp.max, jnp.maximum)
    row_min = reduce_tiles(lambda blk, kt: jnp.where(blk == -jnp.inf, jnp.inf, blk),
                           jnp.full((ACC_ROWS, TQ), jnp.inf, F32), jnp.min, jnp.minimum)
    kf = float(topk)

    def cond(st):
        return jnp.logical_and(st[0] < 400, jnp.min(st[4]) < 0.5)

    def body(st):
        it, lo, hi, thr, done, tie = st
        mid = lo + (hi - lo) * 0.5
        cnt = count(lambda blk, kt: blk >= mid)
        live = done < 0.5
        exact = jnp.logical_and(live, cnt == kf)
        stuck = jnp.logical_and(jnp.logical_and(live, cnt != kf), jnp.logical_or(mid <= lo, mid >= hi))
        thr = jnp.where(exact, mid, jnp.where(stuck, lo, thr))
        tie = jnp.where(stuck, 1.0, tie)
        done = jnp.where(jnp.logical_or(exact, stuck), 1.0, done)
        lo = jnp.where(cnt > kf, mid, lo)
        hi = jnp.where(cnt < kf, mid, hi)
        return it + 1, lo, hi, thr, done, tie

    init = (jnp.int32(0), row_min, row_max + (jnp.abs(row_max) * 1e-6 + 1e-30),
            jnp.full((1, TQ), LOWEST, F32), jnp.where(few, 1.0, 0.0), jnp.zeros((1, TQ), F32))
    _, _, _, thr, _, tie = lax.while_loop(cond, body, init)
    thr_ref[...] = jnp.broadcast_to(thr, thr_ref.shape)

    @pl.when(jnp.max(tie) > 0.5)
    def _():
        is_tie = tie > 0.5
        need = kf - count(lambda blk, kt: blk > thr)

        def kept(limit):
            return count(lambda blk, c: jnp.logical_and(blk == thr, c * CHUNK + key_c <= limit))

        def jbody(_, st):
            jlo, jhi = st
            jmid = (jlo + jhi) // 2
            ok = kept(jmid) >= need
            return jnp.where(ok, jlo, jmid), jnp.where(ok, jmid, jhi)

        jlo0 = jnp.full((1, TQ), -1, jnp.int32)
        jhi0 = jnp.zeros((1, TQ), jnp.int32) + (n_chunks * CHUNK - 1)
        _, cut = lax.fori_loop(0, 14, jbody, (jlo0, jhi0))

        def drop(c, carry):
            blk = sc_ref[chunk_rows(c), :]
            extra = jnp.logical_and(jnp.logical_and(is_tie, blk == thr), c * CHUNK + key_c > cut)
            sc_ref[chunk_rows(c), :] = jnp.where(extra, -jnp.inf, blk)
            return carry

        lax.fori_loop(0, n_chunks, drop, 0)

    _init_softmax(m_ref, l_ref, acc_ref)

    def attend(c, carry):
        rows = chunk_rows(c)
        sel = sc_ref[rows, :] >= thr_ref[0:1, :]

        def logits(h):
            return lax.dot_general(k_ref[0, rows, _head_cols(h)], _head(q_ref, h), _NT,
                                   preferred_element_type=F32)

        def update(h, qk):
            bias = _bias_block(tab_ref, h, (qi * TQ - c * CHUNK) // LANES, DSA_FIRST, DSA_TILES, CHUNK)
            s = qk * (HEAD_DIM ** -0.5 * LOG2E) + bias
            _softmax_step(jnp.where(sel, s, NEG), vt_ref[_head_cols(h), rows], m_ref, l_ref, acc_ref, h)

        _pipelined(N_HEADS, logits, update)
        return carry

    lax.fori_loop(0, n_chunks, attend, 0)
    for h in range(N_HEADS):
        _store_head(o_ref, h, acc_ref[h] / l_ref[h])


def _dsa_attention(pack, vt, w_t, k2, tab):
    bsz, seq, _ = pack.shape
    nq = seq // TQ
    topk = min(TOPK_MAX, seq // 4)
    qb = _q_block(MIX_A)
    return pl.pallas_call(
        functools.partial(_dsa_kernel, topk),
        grid=(bsz, nq),
        in_specs=[pl.BlockSpec((1, TQ, IDX_HEADS * IDX_DIM), lambda b, i: (b, i, 0)),
                  pl.BlockSpec((1, IDX_HEADS, TQ), lambda b, i: (b, 0, i)),
                  pl.BlockSpec((1, seq, 2 * IDX_DIM), lambda b, i: (b, 0, 0)),
                  pl.BlockSpec((1, TQ, BRANCH_WIDTH), lambda b, i: (b, i, qb)),
                  pl.BlockSpec((1, seq, BRANCH_WIDTH), lambda b, i: (b, 0, qb + 1)),
                  pl.BlockSpec((BRANCH_WIDTH, seq), lambda b, i: (MIX_A, b)),
                  pl.BlockSpec(tab.shape, lambda b, i: (0, 0, 0, 0))],
        out_specs=pl.BlockSpec((1, TQ, BRANCH_WIDTH), lambda b, i: (b, i, 0)),
        out_shape=jax.ShapeDtypeStruct((bsz, seq, BRANCH_WIDTH), F32),
        scratch_shapes=[pltpu.VMEM((seq, TQ), F32), pltpu.VMEM((SUBLANES, TQ), F32)] + _softmax_scratch(N_HEADS),
        compiler_params=_cparams(("parallel", "arbitrary")),
        name="dsa_attention",
    )(pack, w_t, k2, pack, pack, vt, tab)


def _merge_kernel(oa_ref, ob_ref, oc_ref, od_ref, z_ref, g0_ref, g1_ref, g2_ref, g3_ref, wb_ref, o_ref):
    merged = None
    for b, (o_ref_b, g_ref) in enumerate(zip((oa_ref, ob_ref, oc_ref, od_ref),
                                             (g0_ref, g1_ref, g2_ref, g3_ref))):
        z = z_ref[:, b * BRANCH_WIDTH:(b + 1) * BRANCH_WIDTH].astype(F32)
        y = o_ref_b[...] * (z * jax.nn.sigmoid(z))
        gate = jax.nn.sigmoid(g_ref[...].astype(F32))
        term = gate * jnp.dot(y.astype(BF16), wb_ref[b], preferred_element_type=F32)
        merged = term if merged is None else merged + term
    o_ref[...] = merged.astype(o_ref.dtype)


def _merge(branches, zg, w_branch, tm=256):
    m = zg.shape[0]
    o_spec = pl.BlockSpec((tm, BRANCH_WIDTH), lambda i: (i, 0))
    g_specs = [pl.BlockSpec((tm, D_MODEL), functools.partial(lambda i, b: (i, 1 + b), b=b))
               for b in range(N_BRANCHES)]
    return pl.pallas_call(
        _merge_kernel,
        grid=(m // tm,),
        in_specs=[o_spec] * N_BRANCHES + [pl.BlockSpec((tm, N_BRANCHES * BRANCH_WIDTH), lambda i: (i, 0))]
        + g_specs + [pl.BlockSpec(w_branch.shape, lambda i: (0, 0, 0))],
        out_specs=pl.BlockSpec((tm, D_MODEL), lambda i: (i, 0)),
        out_shape=jax.ShapeDtypeStruct((m, D_MODEL), BF16),
        compiler_params=_cparams(("parallel",)),
        name="gated_merge",
    )(*branches, zg, zg, zg, zg, zg, w_branch)


def _pack_w_in(w_in):
    sizes = (N_BRANCHES * BRANCH_WIDTH, N_BRANCHES * D_MODEL, 3 * BRANCH_WIDTH, IDX_HEADS * IDX_DIM,
             IDX_DIM, IDX_HEADS, 3 * BRANCH_WIDTH, N_HEADS, 3 * BRANCH_WIDTH, 3 * BRANCH_WIDTH)
    offs = np.concatenate([[0], np.cumsum(sizes)])
    z, gate, a, iq, ik, iw, b, ff, c, d = [w_in[:, :, offs[n]:offs[n + 1]] for n in range(len(sizes))]
    qk = [m[:, :, :2 * BRANCH_WIDTH] for m in (a, b, c, d)]
    v = [m[:, :, 2 * BRANCH_WIDTH:] for m in (a, b, c, d)]
    pad = jnp.zeros(w_in.shape[:2] + (N_MISC - MISC_F - N_HEADS,), w_in.dtype)
    return (jnp.concatenate([iq] + qk, axis=-1).astype(BF16),
            jnp.concatenate(v, axis=-1).astype(BF16),
            jnp.concatenate([z, gate], axis=-1).astype(BF16),
            jnp.concatenate([ik, ik, iw, ff, pad], axis=-1).astype(BF16))


def _layer(x2, bsz, seq, layer, norm_w, w_pack, w_v, w_zg, w_misc, fox_b_f, lqk, subln_w, w_branch, w_out,
           tab_dsa, tab_diff, tab_dil):
    h = _rmsnorm(x2, norm_w, BF16)
    pack = _matmul(h, w_pack, BF16, 1024, 512, name="proj_pack").reshape(bsz, seq, N_PACK)
    vt = _matmul(h, w_v, BF16, 1024, 512, transpose_out=True, name="proj_vt")
    zg = _matmul(h, w_zg, BF16, 1024, 512, name="proj_zg")
    misc = _matmul(h, w_misc, F32, 1024, N_MISC, name="proj_misc").reshape(bsz, seq, N_MISC)

    k2 = misc[:, :, :MISC_W].astype(BF16)
    w_t = jnp.transpose(misc[:, :, MISC_W:MISC_F], (0, 2, 1))
    o_dsa = _dsa_attention(pack, vt, w_t, k2, tab_dsa)

    f_bsh = misc[:, :, MISC_F:MISC_F + N_HEADS]
    cum_bhs = _fox_cum(jnp.transpose(f_bsh, (0, 2, 1)), fox_b_f)
    o_fox = _fox_attention(pack, vt, jnp.transpose(cum_bhs, (0, 2, 1)), cum_bhs)

    lambda_init = 0.8 - 0.6 * math.exp(-0.3 * layer)
    o_diff = _diff_attention(pack, vt, tab_diff, lqk, subln_w, lambda_init)
    o_dil = _dil_attention(pack, vt, tab_dil)

    m = bsz * seq
    branches = [o.reshape(m, BRANCH_WIDTH) for o in (o_dsa, o_fox, o_diff, o_dil)]
    merged = _merge(branches, zg, w_branch)
    return _matmul(merged, w_out, F32, 1024, 512, residual=x2, name="out_proj")


def kernel(x, norm_w, w_in, fox_b_f, diff_lq1, diff_lk1, diff_lq2, diff_lk2, diff_subln_w, w_branch, w_out,
           rel_bias, final_norm_w):
    bsz, seq, d = x.shape
    w_pack, w_v, w_zg, w_misc = _pack_w_in(w_in)
    w_branch_bf = w_branch.astype(BF16)
    w_out_bf = w_out.astype(BF16)
    lqk = jnp.stack([diff_lq1, diff_lk1, diff_lq2, diff_lk2], axis=1)
    bias_dsa, bias_diff, bias_dil = jnp.split(rel_bias, 3, axis=-1)
    tab_dsa = _bias_tiles(bias_dsa, DSA_FIRST, DSA_TILES, causal=False)
    tab_diff = _bias_tiles(bias_diff, DIFF_FIRST, DIFF_TILES, causal=True)
    tab_dil = _dilated_tiles(bias_dil, DIL_FIRST, DIL_TILES)

    x2 = x.reshape(bsz * seq, d)
    for layer in range(DEPTH):
        x2 = _layer(x2, bsz, seq, layer, norm_w[layer], w_pack[layer], w_v[layer], w_zg[layer], w_misc[layer],
                    fox_b_f[layer], lqk[layer], diff_subln_w[layer], w_branch_bf[layer], w_out_bf[layer],
                    tab_dsa, tab_diff, tab_dil)
    return _rmsnorm(x2, final_norm_w, F32).reshape(bsz, seq, d)
```

```python
import functools
import math

import jax
import jax.numpy as jnp
import numpy as np
from jax import lax
from jax.experimental import pallas as pl
from jax.experimental.pallas import tpu as pltpu

F32 = jnp.float32
BF16 = jnp.bfloat16

D_MODEL = 2048
DEPTH = 4
HEAD_DIM = 128
N_HEADS = 4
BRANCH_WIDTH = N_HEADS * HEAD_DIM
N_BRANCHES = 4
DIFF_DIM = HEAD_DIM // 2
IDX_HEADS = 16
IDX_DIM = 64
TOPK_MAX = 256
DILATED_CONFIGS = ((128, 1), (512, 4), (2048, 16))
N_BUCKETS = 32
MAX_DISTANCE = 2048
RMS_EPS = 1e-6

LANES = 128
SUBLANES = 8
TQ = 256
TK = 256
TKF = 512
LOG2E = math.log2(math.e)
PROJ_TM, PROJ_TN = 2048, 512
NEG = -1e30
VMEM_LIMIT = 56 * 1024 * 1024

COL_IDXQ = 0
COL_QK = IDX_HEADS * IDX_DIM
N_PACK = COL_QK + N_BRANCHES * 2 * BRANCH_WIDTH
N_V = N_BRANCHES * BRANCH_WIDTH
N_ZG = N_BRANCHES * BRANCH_WIDTH + N_BRANCHES * D_MODEL
N_MISC = 256
MISC_W = 2 * IDX_DIM
MISC_F = MISC_W + IDX_HEADS
MIX_A, MIX_B, MIX_C, MIX_D = range(N_BRANCHES)

_NT = (((1,), (1,)), ((), ()))
_ONCE_PER_BATCH = pl.Buffered(1)


def _cparams(sem):
    return pltpu.CompilerParams(dimension_semantics=sem, vmem_limit_bytes=VMEM_LIMIT)


def _q_block(mixer):
    return COL_QK // BRANCH_WIDTH + 2 * mixer


def _rmsnorm_kernel(x_ref, w_ref, o_ref):
    x = x_ref[...]
    y = x * lax.rsqrt(jnp.mean(x * x, axis=-1, keepdims=True) + RMS_EPS)
    o_ref[...] = (y * w_ref[...]).astype(o_ref.dtype)


def _rmsnorm(x2, w, out_dtype, tm=512):
    m, d = x2.shape
    return pl.pallas_call(
        _rmsnorm_kernel,
        grid=(m // tm,),
        in_specs=[pl.BlockSpec((tm, d), lambda i: (i, 0)), pl.BlockSpec((1, d), lambda i: (0, 0))],
        out_specs=pl.BlockSpec((tm, d), lambda i: (i, 0)),
        out_shape=jax.ShapeDtypeStruct((m, d), out_dtype),
        compiler_params=_cparams(("parallel",)),
        name="rmsnorm",
    )(x2, w.reshape(1, d))


def _matmul_kernel(a_ref, b_ref, o_ref):
    o_ref[...] = jnp.dot(a_ref[...], b_ref[...], preferred_element_type=F32).astype(o_ref.dtype)


def _matmul_t_kernel(a_ref, b_ref, o_ref):
    o_ref[...] = jnp.dot(a_ref[...], b_ref[...], preferred_element_type=F32).T.astype(o_ref.dtype)


def _matmul_res_kernel(a_ref, b_ref, r_ref, o_ref):
    o_ref[...] = r_ref[...] + jnp.dot(a_ref[...], b_ref[...], preferred_element_type=F32)


def _matmul(a, b, out_dtype, tm, tn, residual=None, transpose_out=False, name="matmul"):
    m, k = a.shape
    n = b.shape[1]
    tm, tn = min(tm, m), min(tn, n)
    in_specs = [pl.BlockSpec((tm, k), lambda i, j: (i, 0)), pl.BlockSpec((k, tn), lambda i, j: (0, j))]
    args = [a, b]
    kern = _matmul_kernel
    out_spec = pl.BlockSpec((tm, tn), lambda i, j: (i, j))
    out_shape = (m, n)
    if residual is not None:
        in_specs.append(pl.BlockSpec((tm, tn), lambda i, j: (i, j)))
        args.append(residual)
        kern = _matmul_res_kernel
    if transpose_out:
        kern = _matmul_t_kernel
        out_spec = pl.BlockSpec((tn, tm), lambda i, j: (j, i))
        out_shape = (n, m)
    return pl.pallas_call(
        kern,
        grid=(m // tm, n // tn),
        in_specs=in_specs,
        out_specs=out_spec,
        out_shape=jax.ShapeDtypeStruct(out_shape, out_dtype),
        compiler_params=_cparams(("parallel", "arbitrary")),
        name=name,
    )(*args)


def _t5_bucket(dist):
    dist = jnp.maximum(dist, 0)
    max_exact = N_BUCKETS // 2
    d = jnp.maximum(dist, max_exact).astype(F32)
    large = max_exact + (jnp.log(d / max_exact) / math.log(MAX_DISTANCE / max_exact)
                         * (N_BUCKETS - max_exact)).astype(jnp.int32)
    large = jnp.minimum(large, N_BUCKETS - 1)
    return jnp.where(dist < max_exact, dist, large)


def _toeplitz_tiles(vals, n_tiles):
    nh, length = vals.shape
    hank = jnp.tile(vals, (1, LANES + 1))[:, :LANES * (length + 1)].reshape(nh, LANES, length + 1)
    hank = hank[:, ::-1, :n_tiles * LANES]
    return jnp.transpose(hank.reshape(nh, LANES, n_tiles, LANES), (0, 2, 1, 3))


def _distances(first, n_tiles):
    return first * LANES - (LANES - 1) + jnp.arange(n_tiles * LANES + LANES - 1)


def _bias_tiles(table, first, n_tiles, causal):
    d = _distances(first, n_tiles)
    vals = table[_t5_bucket(d)].T
    if causal:
        vals = jnp.where(d[None] >= 0, vals, NEG)
    return _toeplitz_tiles(vals * LOG2E, n_tiles)


def _dilated_tiles(table, first, n_tiles):
    d = _distances(first, n_tiles)
    mult = jnp.zeros(d.shape, F32)
    for window, dilation in DILATED_CONFIGS:
        mult = mult + ((d >= 0) & (d <= window) & (d % dilation == 0)).astype(F32)
    vals = table[_t5_bucket(d)].T + jnp.log(jnp.maximum(mult, 1.0))[None]
    return _toeplitz_tiles(jnp.where(mult[None] > 0, vals, NEG) * LOG2E, n_tiles)


def _bias_block(tab_ref, h, a, first, n_tiles, tk=TK):
    def tile(off):
        return tab_ref[h, jnp.clip(off - first, 0, n_tiles - 1)]
    rows = [jnp.concatenate([tile(a + ib - jb) for ib in range(TQ // LANES)], axis=1)
            for jb in range(tk // LANES)]
    return jnp.concatenate(rows, axis=0)


def _softmax_step(s, vt, m_ref, l_ref, acc_ref, idx):
    m_prev = m_ref[idx]
    m_next = jnp.maximum(m_prev, jnp.max(s, axis=0, keepdims=True))
    alpha = jnp.exp2(m_prev - m_next)
    p = jnp.exp2(s - m_next)
    l_ref[idx] = alpha * l_ref[idx] + jnp.sum(p, axis=0, keepdims=True)
    m_ref[idx] = m_next
    acc_ref[idx] = acc_ref[idx] * alpha + jnp.dot(vt, p.astype(BF16), preferred_element_type=F32)


def _pipelined(n, logits_fn, update_fn, lookahead=3):
    pending = {i: logits_fn(i) for i in range(min(lookahead, n))}
    for i in range(n):
        if i + lookahead < n:
            pending[i + lookahead] = logits_fn(i + lookahead)
        update_fn(i, pending.pop(i))


def _init_softmax(m_ref, l_ref, acc_ref):
    m_ref[...] = jnp.full(m_ref.shape, NEG, F32)
    l_ref[...] = jnp.zeros(l_ref.shape, F32)
    acc_ref[...] = jnp.zeros(acc_ref.shape, F32)


def _softmax_scratch(n):
    return [pltpu.VMEM((n, 1, TQ), F32), pltpu.VMEM((n, 1, TQ), F32), pltpu.VMEM((n, HEAD_DIM, TQ), F32)]


def _head(ref, h):
    return ref[0, :, h * HEAD_DIM:(h + 1) * HEAD_DIM]


def _store_head(o_ref, h, o_t):
    o_ref[0, :, h * HEAD_DIM:(h + 1) * HEAD_DIM] = o_t.T


def _logsig_cumsum_kernel(f_ref, b_ref, o_ref):
    x = f_ref[0, 0] + b_ref[0, 0]
    ls = jnp.minimum(x, 0.0) - jnp.log1p(jnp.exp(-jnp.abs(x)))
    rows = ls.shape[0]
    r = lax.broadcasted_iota(jnp.int32, (LANES, LANES), 0)
    c = lax.broadcasted_iota(jnp.int32, (LANES, LANES), 1)
    upper = (r <= c).astype(F32)
    within = jnp.dot(ls, upper, preferred_element_type=F32, precision=lax.Precision.HIGHEST)
    total = jnp.broadcast_to(within[:, LANES - 1:LANES], (rows, LANES))
    rr = lax.broadcasted_iota(jnp.int32, (rows, rows), 0)
    rc = lax.broadcasted_iota(jnp.int32, (rows, rows), 1)
    strict = (rc < rr).astype(F32)
    before = jnp.dot(strict, total, preferred_element_type=F32, precision=lax.Precision.HIGHEST)
    o_ref[0, 0] = within + before


def _fox_cum(f_bhs, b_f):
    bsz, nh, seq = f_bhs.shape
    rows = seq // LANES
    out = pl.pallas_call(
        _logsig_cumsum_kernel,
        grid=(bsz, nh),
        in_specs=[pl.BlockSpec((1, 1, rows, LANES), lambda b, h: (b, h, 0, 0)),
                  pl.BlockSpec((1, 1, 1, LANES), lambda b, h: (0, h, 0, 0))],
        out_specs=pl.BlockSpec((1, 1, rows, LANES), lambda b, h: (b, h, 0, 0)),
        out_shape=jax.ShapeDtypeStruct((bsz, nh, rows, LANES), F32),
        compiler_params=_cparams(("parallel", "parallel")),
        name="fox_cumsum",
    )(f_bhs.reshape(bsz, nh, rows, LANES), jnp.broadcast_to(b_f[None, :, None, None], (1, nh, 1, LANES)))
    return out.reshape(bsz, nh, seq)


def _last_key_tile(qi, tk):
    return (qi * TQ + TQ - 1) // tk


def _key_rows(c, tk):
    return pl.ds(pl.multiple_of(c * tk, tk), tk)


def _head_cols(h):
    return slice(h * HEAD_DIM, (h + 1) * HEAD_DIM)


def _seq_specs(mixer, seq):
    qb = _q_block(mixer)
    return [pl.BlockSpec((1, TQ, BRANCH_WIDTH), lambda b, i: (b, i, qb)),
            pl.BlockSpec((1, seq, BRANCH_WIDTH), lambda b, i: (b, 0, qb + 1), pipeline_mode=_ONCE_PER_BATCH),
            pl.BlockSpec((BRANCH_WIDTH, seq), lambda b, i: (mixer, b), pipeline_mode=_ONCE_PER_BATCH)]


def _fox_kernel(q_ref, k_ref, vt_ref, cq_ref, ck_ref, o_ref, m_ref, l_ref, acc_ref):
    qi = pl.program_id(1)
    last = _last_key_tile(qi, TKF)
    _init_softmax(m_ref, l_ref, acc_ref)

    def chunk(c, diagonal):
        rows = _key_rows(c, TKF)
        if diagonal:
            key = c * TKF + lax.broadcasted_iota(jnp.int32, (TKF, TQ), 0)
            qry = qi * TQ + lax.broadcasted_iota(jnp.int32, (TKF, TQ), 1)
            causal = key <= qry

        def logits(h):
            decay = cq_ref[0, h:h + 1, :] * LOG2E - ck_ref[0, rows, h:h + 1] * LOG2E
            return decay + lax.dot_general(k_ref[0, rows, _head_cols(h)], _head(q_ref, h), _NT,
                                           preferred_element_type=F32)

        def update(h, s):
            if diagonal:
                s = jnp.where(causal, s, NEG)
            _softmax_step(s, vt_ref[_head_cols(h), rows], m_ref, l_ref, acc_ref, h)

        _pipelined(N_HEADS, logits, update)

    def body(c, carry):
        chunk(c, False)
        return carry

    lax.fori_loop(0, last, body, 0)
    chunk(last, True)
    for h in range(N_HEADS):
        _store_head(o_ref, h, acc_ref[h] / l_ref[h])


def _fox_attention(pack, vt, cum_bsh, cum_bhs):
    bsz, seq, _ = pack.shape
    return pl.pallas_call(
        _fox_kernel,
        grid=(bsz, seq // TQ),
        in_specs=_seq_specs(MIX_B, seq)
        + [pl.BlockSpec((1, N_HEADS, TQ), lambda b, i: (b, 0, i)),
           pl.BlockSpec((1, seq, N_HEADS), lambda b, i: (b, 0, 0))],
        out_specs=pl.BlockSpec((1, TQ, BRANCH_WIDTH), lambda b, i: (b, i, 0)),
        out_shape=jax.ShapeDtypeStruct((bsz, seq, BRANCH_WIDTH), F32),
        scratch_shapes=_softmax_scratch(N_HEADS),
        compiler_params=_cparams(("parallel", "arbitrary")),
        name="fox_attention",
    )(pack, pack, vt, cum_bhs, cum_bsh)


DIFF_FIRST, DIFF_TILES = -1, 15


def _diff_kernel(lambda_init, q_ref, k_ref, vt_ref, tab_ref, lqk_ref, sub_ref, o_ref, m_ref, l_ref, acc_ref):
    qi = pl.program_id(1)
    _init_softmax(m_ref, l_ref, acc_ref)
    lane = lax.broadcasted_iota(jnp.int32, (TQ, HEAD_DIM), 1)

    def body(c, carry):
        rows = _key_rows(c, TKF)

        def logits(i):
            h, m = divmod(i, 2)
            q = _head(q_ref, h)
            keep = lane < DIFF_DIM if m == 0 else lane >= DIFF_DIM
            qc = jnp.where(keep, q, jnp.zeros_like(q))
            bias = _bias_block(tab_ref, h, (qi * TQ - c * TKF) // LANES, DIFF_FIRST, DIFF_TILES, TKF)
            return bias + lax.dot_general(k_ref[0, rows, _head_cols(h)], qc, _NT, preferred_element_type=F32)

        def update(i, s):
            _softmax_step(s, vt_ref[_head_cols(i // 2), rows], m_ref, l_ref, acc_ref, i)

        _pipelined(2 * N_HEADS, logits, update)
        return carry

    lax.fori_loop(0, _last_key_tile(qi, TKF) + 1, body, 0)

    lqk = lqk_ref[...]
    lam = (jnp.exp(jnp.sum(lqk[0:1] * lqk[1:2], axis=1, keepdims=True))
           - jnp.exp(jnp.sum(lqk[2:3] * lqk[3:4], axis=1, keepdims=True)) + lambda_init)
    for h in range(N_HEADS):
        o = acc_ref[2 * h] / l_ref[2 * h] - lam * (acc_ref[2 * h + 1] / l_ref[2 * h + 1])
        y = o * lax.rsqrt(jnp.mean(o * o, axis=0, keepdims=True) + RMS_EPS) * sub_ref[...]
        _store_head(o_ref, h, y * (1.0 - lambda_init))


def _diff_attention(pack, vt, tab, lqk, subln_w, lambda_init):
    bsz, seq, _ = pack.shape
    sub = jnp.broadcast_to(subln_w[:, None], (HEAD_DIM, TQ))
    return pl.pallas_call(
        functools.partial(_diff_kernel, lambda_init),
        grid=(bsz, seq // TQ),
        in_specs=_seq_specs(MIX_C, seq)
        + [pl.BlockSpec(tab.shape, lambda b, i: (0, 0, 0, 0)),
           pl.BlockSpec(lqk.shape, lambda b, i: (0, 0)),
           pl.BlockSpec((HEAD_DIM, TQ), lambda b, i: (0, 0))],
        out_specs=pl.BlockSpec((1, TQ, BRANCH_WIDTH), lambda b, i: (b, i, 0)),
        out_shape=jax.ShapeDtypeStruct((bsz, seq, BRANCH_WIDTH), F32),
        scratch_shapes=_softmax_scratch(2 * N_HEADS),
        compiler_params=_cparams(("parallel", "arbitrary")),
        name="diff_attention",
    )(pack, pack, vt, tab, lqk, sub)


DIL_FIRST = -1
DIL_TILES = MAX_DISTANCE // LANES + 1 - DIL_FIRST + 1


def _dil_kernel(q_ref, k_ref, vt_ref, tab_ref, o_ref, m_ref, l_ref, acc_ref):
    qi = pl.program_id(1)
    _init_softmax(m_ref, l_ref, acc_ref)

    def body(c, carry):
        rows = _key_rows(c, TKF)

        def logits(h):
            bias = _bias_block(tab_ref, h, (qi * TQ - c * TKF) // LANES, DIL_FIRST, DIL_TILES, TKF)
            return bias + lax.dot_general(k_ref[0, rows, _head_cols(h)], _head(q_ref, h), _NT,
                                          preferred_element_type=F32)

        def update(h, s):
            _softmax_step(s, vt_ref[_head_cols(h), rows], m_ref, l_ref, acc_ref, h)

        _pipelined(N_HEADS, logits, update)
        return carry

    first = jnp.maximum(qi * TQ - MAX_DISTANCE, 0) // TKF
    lax.fori_loop(first, _last_key_tile(qi, TKF) + 1, body, 0)
    for h in range(N_HEADS):
        _store_head(o_ref, h, acc_ref[h] / l_ref[h])


def _dil_attention(pack, vt, tab):
    bsz, seq, _ = pack.shape
    return pl.pallas_call(
        _dil_kernel,
        grid=(bsz, seq // TQ),
        in_specs=_seq_specs(MIX_D, seq) + [pl.BlockSpec(tab.shape, lambda b, i: (0, 0, 0, 0))],
        out_specs=pl.BlockSpec((1, TQ, BRANCH_WIDTH), lambda b, i: (b, i, 0)),
        out_shape=jax.ShapeDtypeStruct((bsz, seq, BRANCH_WIDTH), F32),
        scratch_shapes=_softmax_scratch(N_HEADS),
        compiler_params=_cparams(("parallel", "arbitrary")),
        name="dilated_attention",
    )(pack, pack, vt, tab)


DSA_FIRST, DSA_TILES = -1, 15
LOWEST = -3.0e38
CHUNK = 2 * TK
ACC_ROWS = 4 * SUBLANES
COARSE_ITERS = 9


def _fold(x, op):
    return op(x.reshape(CHUNK // ACC_ROWS, ACC_ROWS, TQ), axis=0)


def _dsa_kernel(topk, iq_ref, wt_ref, k2_ref, q_ref, k_ref, vt_ref, tab_ref, o_ref,
                sc_ref, scb_ref, thr_ref, m_ref, l_ref, acc_ref):
    qi = pl.program_id(1)
    n_tiles = qi + 1
    lane = lax.broadcasted_iota(jnp.int32, (TQ, 2 * IDX_DIM), 1)
    wt = wt_ref[0] * (IDX_HEADS ** -0.5 * IDX_DIM ** -0.5)
    n_chunks = (n_tiles + 1) // 2
    key_c = lax.broadcasted_iota(jnp.int32, (CHUNK, TQ), 0)
    qry_c = lax.broadcasted_iota(jnp.int32, (CHUNK, TQ), 1)

    def chunk_rows(c):
        return pl.ds(pl.multiple_of(c * CHUNK, CHUNK), CHUNK)

    def score_chunk(c, last):
        k2 = k2_ref[0, chunk_rows(c), :]
        acc = jnp.zeros((CHUNK, TQ), F32)
        for pair in range(IDX_HEADS // 2):
            qp = iq_ref[0, :, pair * 2 * IDX_DIM:(pair + 1) * 2 * IDX_DIM]
            for half in range(2):
                keep = lane < IDX_DIM if half == 0 else lane >= IDX_DIM
                qc = jnp.where(keep, qp, jnp.zeros_like(qp))
                rel = jnp.maximum(lax.dot_general(k2, qc, _NT, preferred_element_type=F32), 0.0)
                hh = 2 * pair + half
                acc = acc + wt[hh:hh + 1, :] * rel
        if last:
            acc = jnp.where(c * CHUNK + key_c <= qi * TQ + qry_c, acc, -jnp.inf)
        sc_ref[chunk_rows(c), :] = acc
        scb_ref[chunk_rows(c), :] = acc.astype(BF16)

    def score_body(c, carry):
        score_chunk(c, False)
        return carry

    lax.fori_loop(0, n_chunks - 1, score_body, 0)
    score_chunk(n_chunks - 1, True)

    def reduce_tiles(fn, init, reduce, combine):
        def body(c, carry):
            return combine(carry, _fold(fn(sc_ref[chunk_rows(c), :], c), reduce))
        return reduce(lax.fori_loop(0, n_chunks, body, init), axis=0, keepdims=True)

    def count(pred):
        return reduce_tiles(lambda blk, kt: jnp.where(pred(blk, kt), 1.0, 0.0),
                            jnp.zeros((ACC_ROWS, TQ), F32), jnp.sum, jnp.add)

    t_q = qi * TQ + lax.broadcasted_iota(jnp.int32, (1, TQ), 1)
    few = t_q < topk
    row_max = reduce_tiles(lambda blk, kt: blk, jnp.full((ACC_ROWS, TQ), -jnp.inf, F32), jnp.max, jnp.maximum)
    row_min = reduce_tiles(lambda blk, kt: jnp.where(blk == -jnp.inf, jnp.inf, blk),
                           jnp.full((ACC_ROWS, TQ), jnp.inf, F32), jnp.min, jnp.minimum)
    kf = float(topk)

    def count_b(mid_b):
        def body(c, carry):
            blk = scb_ref[chunk_rows(c), :]
            ind = jnp.where(blk >= mid_b, jnp.ones_like(blk), jnp.zeros_like(blk))
            parts = ind.reshape(CHUNK // ACC_ROWS, ACC_ROWS, TQ)
            for g in range(CHUNK // ACC_ROWS):
                carry = carry + parts[g]
            return carry
        acc = lax.fori_loop(0, n_chunks, body, jnp.zeros((ACC_ROWS, TQ), BF16))
        return jnp.sum(acc.astype(F32), axis=0, keepdims=True)

    def coarse(_, st):
        lo, hi = st
        mid_b = (lo + (hi - lo) * 0.5).astype(BF16)
        ge = count_b(mid_b) >= kf
        mid = mid_b.astype(F32)
        return jnp.where(ge, mid, lo), jnp.where(ge, hi, mid)

    lo_b = row_min.astype(BF16).astype(F32)
    hi_b = (row_max + (jnp.abs(row_max) * 2.0 ** -6 + 1e-30)).astype(BF16).astype(F32)
    lo_b, hi_b = lax.fori_loop(0, COARSE_ITERS, coarse, (lo_b, hi_b))
    lo0 = lo_b - (jnp.abs(lo_b) * 2.0 ** -8 + 1e-30)
    hi0 = hi_b + (jnp.abs(hi_b) * 2.0 ** -8 + 1e-30)

    def cond(st):
        return jnp.logical_and(st[0] < 400, jnp.min(st[4]) < 0.5)

    def body(st):
        it, lo, hi, thr, done, tie = st
        mid = lo + (hi - lo) * 0.5
        cnt = count(lambda blk, kt: blk >= mid)
        live = done < 0.5
        exact = jnp.logical_and(live, cnt == kf)
        stuck = jnp.logical_and(jnp.logical_and(live, cnt != kf), jnp.logical_or(mid <= lo, mid >= hi))
        thr = jnp.where(exact, mid, jnp.where(stuck, lo, thr))
        tie = jnp.where(stuck, 1.0, tie)
        done = jnp.where(jnp.logical_or(exact, stuck), 1.0, done)
        lo = jnp.where(cnt > kf, mid, lo)
        hi = jnp.where(cnt < kf, mid, hi)
        return it + 1, lo, hi, thr, done, tie

    init = (jnp.int32(0), lo0, hi0,
            jnp.full((1, TQ), LOWEST, F32), jnp.where(few, 1.0, 0.0), jnp.zeros((1, TQ), F32))
    _, _, _, thr, _, tie = lax.while_loop(cond, body, init)
    thr_ref[...] = jnp.broadcast_to(thr, thr_ref.shape)

    @pl.when(jnp.max(tie) > 0.5)
    def _():
        is_tie = tie > 0.5
        need = kf - count(lambda blk, kt: blk > thr)

        def kept(limit):
            return count(lambda blk, c: jnp.logical_and(blk == thr, c * CHUNK + key_c <= limit))

        def jbody(_, st):
            jlo, jhi = st
            jmid = (jlo + jhi) // 2
            ok = kept(jmid) >= need
            return jnp.where(ok, jlo, jmid), jnp.where(ok, jmid, jhi)

        jlo0 = jnp.full((1, TQ), -1, jnp.int32)
        jhi0 = jnp.zeros((1, TQ), jnp.int32) + (n_chunks * CHUNK - 1)
        _, cut = lax.fori_loop(0, 14, jbody, (jlo0, jhi0))

        def drop(c, carry):
            blk = sc_ref[chunk_rows(c), :]
            extra = jnp.logical_and(jnp.logical_and(is_tie, blk == thr), c * CHUNK + key_c > cut)
            sc_ref[chunk_rows(c), :] = jnp.where(extra, -jnp.inf, blk)
            return carry

        lax.fori_loop(0, n_chunks, drop, 0)

    _init_softmax(m_ref, l_ref, acc_ref)

    def attend(c, carry):
        rows = chunk_rows(c)
        sel = sc_ref[rows, :] >= thr_ref[0:1, :]

        def logits(h):
            bias = _bias_block(tab_ref, h, (qi * TQ - c * CHUNK) // LANES, DSA_FIRST, DSA_TILES, CHUNK)
            return bias + lax.dot_general(k_ref[0, rows, _head_cols(h)], _head(q_ref, h), _NT,
                                          preferred_element_type=F32)

        def update(h, s):
            _softmax_step(jnp.where(sel, s, NEG), vt_ref[_head_cols(h), rows], m_ref, l_ref, acc_ref, h)

        _pipelined(N_HEADS, logits, update)
        return carry

    lax.fori_loop(0, n_chunks, attend, 0)
    for h in range(N_HEADS):
        _store_head(o_ref, h, acc_ref[h] / l_ref[h])


def _dsa_attention(pack, vt, w_t, k2, tab):
    bsz, seq, _ = pack.shape
    nq = seq // TQ
    topk = min(TOPK_MAX, seq // 4)
    qb = _q_block(MIX_A)
    return pl.pallas_call(
        functools.partial(_dsa_kernel, topk),
        grid=(bsz, nq),
        in_specs=[pl.BlockSpec((1, TQ, IDX_HEADS * IDX_DIM), lambda b, i: (b, i, 0)),
                  pl.BlockSpec((1, IDX_HEADS, TQ), lambda b, i: (b, 0, i)),
                  pl.BlockSpec((1, seq, 2 * IDX_DIM), lambda b, i: (b, 0, 0), pipeline_mode=_ONCE_PER_BATCH),
                  pl.BlockSpec((1, TQ, BRANCH_WIDTH), lambda b, i: (b, i, qb)),
                  pl.BlockSpec((1, seq, BRANCH_WIDTH), lambda b, i: (b, 0, qb + 1), pipeline_mode=_ONCE_PER_BATCH),
                  pl.BlockSpec((BRANCH_WIDTH, seq), lambda b, i: (MIX_A, b), pipeline_mode=_ONCE_PER_BATCH),
                  pl.BlockSpec(tab.shape, lambda b, i: (0, 0, 0, 0))],
        out_specs=pl.BlockSpec((1, TQ, BRANCH_WIDTH), lambda b, i: (b, i, 0)),
        out_shape=jax.ShapeDtypeStruct((bsz, seq, BRANCH_WIDTH), F32),
        scratch_shapes=[pltpu.VMEM((seq, TQ), F32), pltpu.VMEM((seq, TQ), BF16), pltpu.VMEM((SUBLANES, TQ), F32)]
        + _softmax_scratch(N_HEADS),
        compiler_params=_cparams(("parallel", "arbitrary")),
        name="dsa_attention",
    )(pack, w_t, k2, pack, pack, vt, tab)


def _merge_kernel(oa_ref, ob_ref, oc_ref, od_ref, z_ref, g0_ref, g1_ref, g2_ref, g3_ref, wb_ref, o_ref):
    merged = None
    for b, (o_ref_b, g_ref) in enumerate(zip((oa_ref, ob_ref, oc_ref, od_ref),
                                             (g0_ref, g1_ref, g2_ref, g3_ref))):
        z = z_ref[:, b * BRANCH_WIDTH:(b + 1) * BRANCH_WIDTH].astype(F32)
        y = o_ref_b[...] * (z * jax.nn.sigmoid(z))
        gate = jax.nn.sigmoid(g_ref[...].astype(F32))
        term = gate * jnp.dot(y.astype(BF16), wb_ref[b], preferred_element_type=F32)
        merged = term if merged is None else merged + term
    o_ref[...] = merged.astype(o_ref.dtype)


def _merge(branches, zg, w_branch, tm=256):
    m = zg.shape[0]
    o_spec = pl.BlockSpec((tm, BRANCH_WIDTH), lambda i: (i, 0))
    g_specs = [pl.BlockSpec((tm, D_MODEL), functools.partial(lambda i, b: (i, 1 + b), b=b))
               for b in range(N_BRANCHES)]
    return pl.pallas_call(
        _merge_kernel,
        grid=(m // tm,),
        in_specs=[o_spec] * N_BRANCHES + [pl.BlockSpec((tm, N_BRANCHES * BRANCH_WIDTH), lambda i: (i, 0))]
        + g_specs + [pl.BlockSpec(w_branch.shape, lambda i: (0, 0, 0))],
        out_specs=pl.BlockSpec((tm, D_MODEL), lambda i: (i, 0)),
        out_shape=jax.ShapeDtypeStruct((m, D_MODEL), BF16),
        compiler_params=_cparams(("parallel",)),
        name="gated_merge",
    )(*branches, zg, zg, zg, zg, zg, w_branch)


def _pack_w_in(w_in):
    sizes = (N_BRANCHES * BRANCH_WIDTH, N_BRANCHES * D_MODEL, 3 * BRANCH_WIDTH, IDX_HEADS * IDX_DIM,
             IDX_DIM, IDX_HEADS, 3 * BRANCH_WIDTH, N_HEADS, 3 * BRANCH_WIDTH, 3 * BRANCH_WIDTH)
    offs = np.concatenate([[0], np.cumsum(sizes)])
    z, gate, a, iq, ik, iw, b, ff, c, d = [w_in[:, :, offs[n]:offs[n + 1]] for n in range(len(sizes))]
    scales = (HEAD_DIM ** -0.5, HEAD_DIM ** -0.5, DIFF_DIM ** -0.5, HEAD_DIM ** -0.5)
    qk = [jnp.concatenate([m[:, :, :BRANCH_WIDTH] * (sc * LOG2E), m[:, :, BRANCH_WIDTH:2 * BRANCH_WIDTH]], axis=-1)
          for m, sc in zip((a, b, c, d), scales)]
    v = [m[:, :, 2 * BRANCH_WIDTH:] for m in (a, b, c, d)]
    pad = jnp.zeros(w_in.shape[:2] + (N_MISC - MISC_F - N_HEADS,), w_in.dtype)
    return (jnp.concatenate([iq] + qk, axis=-1).astype(BF16),
            jnp.concatenate(v, axis=-1).astype(BF16),
            jnp.concatenate([z, gate], axis=-1).astype(BF16),
            jnp.concatenate([ik, ik, iw, ff, pad], axis=-1).astype(BF16))


def _layer(x2, bsz, seq, layer, norm_w, w_pack, w_v, w_zg, w_misc, fox_b_f, lqk, subln_w, w_branch, w_out,
           tab_dsa, tab_diff, tab_dil):
    h = _rmsnorm(x2, norm_w, BF16)
    pack = _matmul(h, w_pack, BF16, PROJ_TM, PROJ_TN, name="proj_pack").reshape(bsz, seq, N_PACK)
    vt = _matmul(h, w_v, BF16, PROJ_TM, PROJ_TN, transpose_out=True, name="proj_vt")
    zg = _matmul(h, w_zg, BF16, PROJ_TM, PROJ_TN, name="proj_zg")
    misc = _matmul(h, w_misc, F32, PROJ_TM, N_MISC, name="proj_misc").reshape(bsz, seq, N_MISC)

    k2 = misc[:, :, :MISC_W].astype(BF16)
    w_t = jnp.transpose(misc[:, :, MISC_W:MISC_F], (0, 2, 1))
    o_dsa = _dsa_attention(pack, vt, w_t, k2, tab_dsa)

    f_bsh = misc[:, :, MISC_F:MISC_F + N_HEADS]
    cum_bhs = _fox_cum(jnp.transpose(f_bsh, (0, 2, 1)), fox_b_f)
    o_fox = _fox_attention(pack, vt, jnp.transpose(cum_bhs, (0, 2, 1)), cum_bhs)

    lambda_init = 0.8 - 0.6 * math.exp(-0.3 * layer)
    o_diff = _diff_attention(pack, vt, tab_diff, lqk, subln_w, lambda_init)
    o_dil = _dil_attention(pack, vt, tab_dil)

    m = bsz * seq
    branches = [o.reshape(m, BRANCH_WIDTH) for o in (o_dsa, o_fox, o_diff, o_dil)]
    merged = _merge(branches, zg, w_branch)
    return _matmul(merged, w_out, F32, PROJ_TM, PROJ_TN, residual=x2, name="out_proj")


def kernel(x, norm_w, w_in, fox_b_f, diff_lq1, diff_lk1, diff_lq2, diff_lk2, diff_subln_w, w_branch, w_out,
           rel_bias, final_norm_w):
    bsz, seq, d = x.shape
    w_pack, w_v, w_zg, w_misc = _pack_w_in(w_in)
    w_branch_bf = w_branch.astype(BF16)
    w_out_bf = w_out.astype(BF16)
    lqk = jnp.stack([diff_lq1, diff_lk1, diff_lq2, diff_lk2], axis=1)
    bias_dsa, bias_diff, bias_dil = jnp.split(rel_bias, 3, axis=-1)
    tab_dsa = _bias_tiles(bias_dsa, DSA_FIRST, DSA_TILES, causal=False)
    tab_diff = _bias_tiles(bias_diff, DIFF_FIRST, DIFF_TILES, causal=True)
    tab_dil = _dilated_tiles(bias_dil, DIL_FIRST, DIL_TILES)

    x2 = x.reshape(bsz * seq, d)
    for layer in range(DEPTH):
        x2 = _layer(x2, bsz, seq, layer, norm_w[layer], w_pack[layer], w_v[layer], w_zg[layer], w_misc[layer],
                    fox_b_f[layer], lqk[layer], diff_subln_w[layer], w_branch_bf[layer], w_out_bf[layer],
                    tab_dsa, tab_diff, tab_dil)
    return _rmsnorm(x2, final_norm_w, F32).reshape(bsz, seq, d)
```

```python
import functools
import math

import jax
import jax.numpy as jnp
import numpy as np
from jax import lax
from jax.experimental import pallas as pl
from jax.experimental.pallas import tpu as pltpu

F32 = jnp.float32
BF16 = jnp.bfloat16

D_MODEL = 2048
DEPTH = 4
HEAD_DIM = 128
N_HEADS = 4
BRANCH_WIDTH = N_HEADS * HEAD_DIM
N_BRANCHES = 4
DIFF_DIM = HEAD_DIM // 2
IDX_HEADS = 16
IDX_DIM = 64
TOPK_MAX = 256
DILATED_CONFIGS = ((128, 1), (512, 4), (2048, 16))
N_BUCKETS = 32
MAX_DISTANCE = 2048
RMS_EPS = 1e-6

LANES = 128
SUBLANES = 8
ONES_ROWS = 16
TQ = 256
TK = 256
TKF = 512
LOG2E = math.log2(math.e)
PROJ_TM, PROJ_TN = 2048, 512
NEG = -1e30
VMEM_LIMIT = 56 * 1024 * 1024

COL_IDXQ = 0
COL_QK = IDX_HEADS * IDX_DIM
N_PACK = COL_QK + N_BRANCHES * 2 * BRANCH_WIDTH
N_V = N_BRANCHES * BRANCH_WIDTH
N_ZG = N_BRANCHES * BRANCH_WIDTH + N_BRANCHES * D_MODEL
N_MISC = 256
MISC_W = 2 * IDX_DIM
MISC_F = MISC_W + IDX_HEADS
MIX_A, MIX_B, MIX_C, MIX_D = range(N_BRANCHES)

_NT = (((1,), (1,)), ((), ()))
_ONCE_PER_BATCH = pl.Buffered(1)


def _cparams(sem):
    return pltpu.CompilerParams(dimension_semantics=sem, vmem_limit_bytes=VMEM_LIMIT)


def _q_block(mixer):
    return COL_QK // BRANCH_WIDTH + 2 * mixer


def _rmsnorm_kernel(x_ref, w_ref, o_ref):
    x = x_ref[...]
    y = x * lax.rsqrt(jnp.mean(x * x, axis=-1, keepdims=True) + RMS_EPS)
    o_ref[...] = (y * w_ref[...]).astype(o_ref.dtype)


def _rmsnorm(x2, w, out_dtype, tm=512):
    m, d = x2.shape
    return pl.pallas_call(
        _rmsnorm_kernel,
        grid=(m // tm,),
        in_specs=[pl.BlockSpec((tm, d), lambda i: (i, 0)), pl.BlockSpec((1, d), lambda i: (0, 0))],
        out_specs=pl.BlockSpec((tm, d), lambda i: (i, 0)),
        out_shape=jax.ShapeDtypeStruct((m, d), out_dtype),
        compiler_params=_cparams(("parallel",)),
        name="rmsnorm",
    )(x2, w.reshape(1, d))


def _matmul_kernel(a_ref, b_ref, o_ref):
    o_ref[...] = jnp.dot(a_ref[...], b_ref[...], preferred_element_type=F32).astype(o_ref.dtype)


def _matmul_t_kernel(a_ref, b_ref, o_ref):
    o_ref[...] = jnp.dot(a_ref[...], b_ref[...], preferred_element_type=F32).T.astype(o_ref.dtype)


def _matmul_res_kernel(a_ref, b_ref, r_ref, o_ref):
    o_ref[...] = r_ref[...] + jnp.dot(a_ref[...], b_ref[...], preferred_element_type=F32)


def _matmul_scale_kernel(a_ref, b_ref, s_ref, o_ref):
    o_ref[...] = (jnp.dot(a_ref[...], b_ref[...], preferred_element_type=F32) * s_ref[...]).astype(o_ref.dtype)


def _matmul(a, b, out_dtype, tm, tn, residual=None, col_scale=None, transpose_out=False, name="matmul"):
    m, k = a.shape
    n = b.shape[1]
    tm, tn = min(tm, m), min(tn, n)
    in_specs = [pl.BlockSpec((tm, k), lambda i, j: (i, 0)), pl.BlockSpec((k, tn), lambda i, j: (0, j))]
    args = [a, b]
    kern = _matmul_kernel
    out_spec = pl.BlockSpec((tm, tn), lambda i, j: (i, j))
    out_shape = (m, n)
    if residual is not None:
        in_specs.append(pl.BlockSpec((tm, tn), lambda i, j: (i, j)))
        args.append(residual)
        kern = _matmul_res_kernel
    if col_scale is not None:
        in_specs.append(pl.BlockSpec((1, tn), lambda i, j: (0, j)))
        args.append(col_scale)
        kern = _matmul_scale_kernel
    if transpose_out:
        kern = _matmul_t_kernel
        out_spec = pl.BlockSpec((tn, tm), lambda i, j: (j, i))
        out_shape = (n, m)
    return pl.pallas_call(
        kern,
        grid=(m // tm, n // tn),
        in_specs=in_specs,
        out_specs=out_spec,
        out_shape=jax.ShapeDtypeStruct(out_shape, out_dtype),
        compiler_params=_cparams(("parallel", "arbitrary")),
        name=name,
    )(*args)


def _t5_bucket(dist):
    dist = jnp.maximum(dist, 0)
    max_exact = N_BUCKETS // 2
    d = jnp.maximum(dist, max_exact).astype(F32)
    large = max_exact + (jnp.log(d / max_exact) / math.log(MAX_DISTANCE / max_exact)
                         * (N_BUCKETS - max_exact)).astype(jnp.int32)
    large = jnp.minimum(large, N_BUCKETS - 1)
    return jnp.where(dist < max_exact, dist, large)


def _toeplitz_tiles(vals, n_tiles):
    nh, length = vals.shape
    hank = jnp.tile(vals, (1, LANES + 1))[:, :LANES * (length + 1)].reshape(nh, LANES, length + 1)
    hank = hank[:, ::-1, :n_tiles * LANES]
    return jnp.transpose(hank.reshape(nh, LANES, n_tiles, LANES), (0, 2, 1, 3))


def _distances(first, n_tiles):
    return first * LANES - (LANES - 1) + jnp.arange(n_tiles * LANES + LANES - 1)


def _bias_tiles(table, first, n_tiles, causal):
    d = _distances(first, n_tiles)
    vals = table[_t5_bucket(d)].T
    if causal:
        vals = jnp.where(d[None] >= 0, vals, NEG)
    return _toeplitz_tiles(vals * LOG2E, n_tiles)


def _dilated_tiles(table, first, n_tiles):
    d = _distances(first, n_tiles)
    mult = jnp.zeros(d.shape, F32)
    for window, dilation in DILATED_CONFIGS:
        mult = mult + ((d >= 0) & (d <= window) & (d % dilation == 0)).astype(F32)
    vals = table[_t5_bucket(d)].T + jnp.log(jnp.maximum(mult, 1.0))[None]
    return _toeplitz_tiles(jnp.where(mult[None] > 0, vals, NEG) * LOG2E, n_tiles)


def _bias_block(tab_ref, h, a, first, n_tiles, tk=TK):
    def tile(off):
        return tab_ref[h, jnp.clip(off - first, 0, n_tiles - 1)]
    rows = [jnp.concatenate([tile(a + ib - jb) for ib in range(TQ // LANES)], axis=1)
            for jb in range(tk // LANES)]
    return jnp.concatenate(rows, axis=0)


def _softmax_step(s, vt, m_ref, acc_ref, idx):
    m_prev = m_ref[idx]
    m_next = jnp.maximum(m_prev, jnp.max(s, axis=0, keepdims=True))
    alpha = jnp.exp2(m_prev - m_next)
    p = jnp.exp2(s - m_next).astype(BF16)
    vt_ones = jnp.concatenate([vt, jnp.ones((ONES_ROWS, vt.shape[1]), vt.dtype)], axis=0)
    m_ref[idx] = m_next
    acc_ref[idx] = acc_ref[idx] * alpha + jnp.dot(vt_ones, p, preferred_element_type=F32)


def _normalized(acc_ref, idx):
    acc = acc_ref[idx]
    return acc[:HEAD_DIM] / acc[HEAD_DIM:HEAD_DIM + 1]


def _pipelined(n, logits_fn, update_fn, lookahead=3):
    pending = {i: logits_fn(i) for i in range(min(lookahead, n))}
    for i in range(n):
        if i + lookahead < n:
            pending[i + lookahead] = logits_fn(i + lookahead)
        update_fn(i, pending.pop(i))


def _init_softmax(m_ref, acc_ref):
    m_ref[...] = jnp.full(m_ref.shape, NEG, F32)
    acc_ref[...] = jnp.zeros(acc_ref.shape, F32)


def _softmax_scratch(n):
    return [pltpu.VMEM((n, 1, TQ), F32), pltpu.VMEM((n, HEAD_DIM + ONES_ROWS, TQ), F32)]


def _head(ref, h):
    return ref[0, :, h * HEAD_DIM:(h + 1) * HEAD_DIM]


def _store_head(o_ref, h, o_t):
    o_ref[0, :, h * HEAD_DIM:(h + 1) * HEAD_DIM] = o_t.T


def _logsig_cumsum_kernel(f_ref, b_ref, o_ref):
    x = f_ref[0, 0] + b_ref[0, 0]
    ls = jnp.minimum(x, 0.0) - jnp.log1p(jnp.exp(-jnp.abs(x)))
    rows = ls.shape[0]
    r = lax.broadcasted_iota(jnp.int32, (LANES, LANES), 0)
    c = lax.broadcasted_iota(jnp.int32, (LANES, LANES), 1)
    upper = (r <= c).astype(F32)
    within = jnp.dot(ls, upper, preferred_element_type=F32, precision=lax.Precision.HIGHEST)
    total = jnp.broadcast_to(within[:, LANES - 1:LANES], (rows, LANES))
    rr = lax.broadcasted_iota(jnp.int32, (rows, rows), 0)
    rc = lax.broadcasted_iota(jnp.int32, (rows, rows), 1)
    strict = (rc < rr).astype(F32)
    before = jnp.dot(strict, total, preferred_element_type=F32, precision=lax.Precision.HIGHEST)
    o_ref[0, 0] = within + before


def _fox_cum(f_bhs, b_f):
    bsz, nh, seq = f_bhs.shape
    rows = seq // LANES
    out = pl.pallas_call(
        _logsig_cumsum_kernel,
        grid=(bsz, nh),
        in_specs=[pl.BlockSpec((1, 1, rows, LANES), lambda b, h: (b, h, 0, 0)),
                  pl.BlockSpec((1, 1, 1, LANES), lambda b, h: (0, h, 0, 0))],
        out_specs=pl.BlockSpec((1, 1, rows, LANES), lambda b, h: (b, h, 0, 0)),
        out_shape=jax.ShapeDtypeStruct((bsz, nh, rows, LANES), F32),
        compiler_params=_cparams(("parallel", "parallel")),
        name="fox_cumsum",
    )(f_bhs.reshape(bsz, nh, rows, LANES), jnp.broadcast_to(b_f[None, :, None, None], (1, nh, 1, LANES)))
    return out.reshape(bsz, nh, seq)


def _last_key_tile(qi, tk):
    return (qi * TQ + TQ - 1) // tk


def _key_rows(c, tk):
    return pl.ds(pl.multiple_of(c * tk, tk), tk)


def _head_cols(h):
    return slice(h * HEAD_DIM, (h + 1) * HEAD_DIM)


def _seq_specs(mixer, seq):
    qb = _q_block(mixer)
    return [pl.BlockSpec((1, TQ, BRANCH_WIDTH), lambda b, i: (b, i, qb)),
            pl.BlockSpec((1, seq, BRANCH_WIDTH), lambda b, i: (b, 0, qb + 1), pipeline_mode=_ONCE_PER_BATCH),
            pl.BlockSpec((BRANCH_WIDTH, seq), lambda b, i: (mixer, b), pipeline_mode=_ONCE_PER_BATCH)]


def _fox_kernel(q_ref, k_ref, vt_ref, cq_ref, ck_ref, o_ref, m_ref, acc_ref):
    qi = pl.program_id(1)
    last = _last_key_tile(qi, TKF)
    _init_softmax(m_ref, acc_ref)

    def chunk(c, diagonal):
        rows = _key_rows(c, TKF)
        if diagonal:
            key = c * TKF + lax.broadcasted_iota(jnp.int32, (TKF, TQ), 0)
            qry = qi * TQ + lax.broadcasted_iota(jnp.int32, (TKF, TQ), 1)
            causal = key <= qry

        def logits(h):
            decay = cq_ref[0, h:h + 1, :] * LOG2E - ck_ref[0, rows, h:h + 1] * LOG2E
            return decay + lax.dot_general(k_ref[0, rows, _head_cols(h)], _head(q_ref, h), _NT,
                                           preferred_element_type=F32)

        def update(h, s):
            if diagonal:
                s = jnp.where(causal, s, NEG)
            _softmax_step(s, vt_ref[_head_cols(h), rows], m_ref, acc_ref, h)

        _pipelined(N_HEADS, logits, update)

    def body(c, carry):
        chunk(c, False)
        return carry

    lax.fori_loop(0, last, body, 0)
    chunk(last, True)
    for h in range(N_HEADS):
        _store_head(o_ref, h, _normalized(acc_ref, h))


def _fox_attention(pack, vt, cum_bsh, cum_bhs):
    bsz, seq, _ = pack.shape
    return pl.pallas_call(
        _fox_kernel,
        grid=(bsz, seq // TQ),
        in_specs=_seq_specs(MIX_B, seq)
        + [pl.BlockSpec((1, N_HEADS, TQ), lambda b, i: (b, 0, i)),
           pl.BlockSpec((1, seq, N_HEADS), lambda b, i: (b, 0, 0))],
        out_specs=pl.BlockSpec((1, TQ, BRANCH_WIDTH), lambda b, i: (b, i, 0)),
        out_shape=jax.ShapeDtypeStruct((bsz, seq, BRANCH_WIDTH), F32),
        scratch_shapes=_softmax_scratch(N_HEADS),
        compiler_params=_cparams(("parallel", "arbitrary")),
        name="fox_attention",
    )(pack, pack, vt, cum_bhs, cum_bsh)


DIFF_FIRST, DIFF_TILES = -1, 15


def _diff_kernel(lambda_init, q_ref, k_ref, vt_ref, tab_ref, lqk_ref, sub_ref, o_ref, m_ref, acc_ref):
    qi = pl.program_id(1)
    _init_softmax(m_ref, acc_ref)
    lane = lax.broadcasted_iota(jnp.int32, (TQ, HEAD_DIM), 1)

    def body(c, carry):
        rows = _key_rows(c, TKF)

        def logits(i):
            h, m = divmod(i, 2)
            q = _head(q_ref, h)
            keep = lane < DIFF_DIM if m == 0 else lane >= DIFF_DIM
            qc = jnp.where(keep, q, jnp.zeros_like(q))
            bias = _bias_block(tab_ref, h, (qi * TQ - c * TKF) // LANES, DIFF_FIRST, DIFF_TILES, TKF)
            return bias + lax.dot_general(k_ref[0, rows, _head_cols(h)], qc, _NT, preferred_element_type=F32)

        def update(i, s):
            _softmax_step(s, vt_ref[_head_cols(i // 2), rows], m_ref, acc_ref, i)

        _pipelined(2 * N_HEADS, logits, update)
        return carry

    lax.fori_loop(0, _last_key_tile(qi, TKF) + 1, body, 0)

    lqk = lqk_ref[...]
    lam = (jnp.exp(jnp.sum(lqk[0:1] * lqk[1:2], axis=1, keepdims=True))
           - jnp.exp(jnp.sum(lqk[2:3] * lqk[3:4], axis=1, keepdims=True)) + lambda_init)
    for h in range(N_HEADS):
        o = _normalized(acc_ref, 2 * h) - lam * _normalized(acc_ref, 2 * h + 1)
        y = o * lax.rsqrt(jnp.mean(o * o, axis=0, keepdims=True) + RMS_EPS) * sub_ref[...]
        _store_head(o_ref, h, y * (1.0 - lambda_init))


def _diff_attention(pack, vt, tab, lqk, subln_w, lambda_init):
    bsz, seq, _ = pack.shape
    sub = jnp.broadcast_to(subln_w[:, None], (HEAD_DIM, TQ))
    return pl.pallas_call(
        functools.partial(_diff_kernel, lambda_init),
        grid=(bsz, seq // TQ),
        in_specs=_seq_specs(MIX_C, seq)
        + [pl.BlockSpec(tab.shape, lambda b, i: (0, 0, 0, 0)),
           pl.BlockSpec(lqk.shape, lambda b, i: (0, 0)),
           pl.BlockSpec((HEAD_DIM, TQ), lambda b, i: (0, 0))],
        out_specs=pl.BlockSpec((1, TQ, BRANCH_WIDTH), lambda b, i: (b, i, 0)),
        out_shape=jax.ShapeDtypeStruct((bsz, seq, BRANCH_WIDTH), F32),
        scratch_shapes=_softmax_scratch(2 * N_HEADS),
        compiler_params=_cparams(("parallel", "arbitrary")),
        name="diff_attention",
    )(pack, pack, vt, tab, lqk, sub)


DIL_FIRST = -1
DIL_TILES = MAX_DISTANCE // LANES + 1 - DIL_FIRST + 1


def _dil_kernel(q_ref, k_ref, vt_ref, tab_ref, o_ref, m_ref, acc_ref):
    qi = pl.program_id(1)
    _init_softmax(m_ref, acc_ref)

    def body(c, carry):
        rows = _key_rows(c, TKF)

        def logits(h):
            bias = _bias_block(tab_ref, h, (qi * TQ - c * TKF) // LANES, DIL_FIRST, DIL_TILES, TKF)
            return bias + lax.dot_general(k_ref[0, rows, _head_cols(h)], _head(q_ref, h), _NT,
                                          preferred_element_type=F32)

        def update(h, s):
            _softmax_step(s, vt_ref[_head_cols(h), rows], m_ref, acc_ref, h)

        _pipelined(N_HEADS, logits, update)
        return carry

    first = jnp.maximum(qi * TQ - MAX_DISTANCE, 0) // TKF
    lax.fori_loop(first, _last_key_tile(qi, TKF) + 1, body, 0)
    for h in range(N_HEADS):
        _store_head(o_ref, h, _normalized(acc_ref, h))


def _dil_attention(pack, vt, tab):
    bsz, seq, _ = pack.shape
    return pl.pallas_call(
        _dil_kernel,
        grid=(bsz, seq // TQ),
        in_specs=_seq_specs(MIX_D, seq) + [pl.BlockSpec(tab.shape, lambda b, i: (0, 0, 0, 0))],
        out_specs=pl.BlockSpec((1, TQ, BRANCH_WIDTH), lambda b, i: (b, i, 0)),
        out_shape=jax.ShapeDtypeStruct((bsz, seq, BRANCH_WIDTH), F32),
        scratch_shapes=_softmax_scratch(N_HEADS),
        compiler_params=_cparams(("parallel", "arbitrary")),
        name="dilated_attention",
    )(pack, pack, vt, tab)


DSA_FIRST, DSA_TILES = -1, 15
LOWEST = -3.0e38
CHUNK = 2 * TK
ACC_ROWS = 4 * SUBLANES
COARSE_ITERS = 9


def _fold(x, op):
    return op(x.reshape(CHUNK // ACC_ROWS, ACC_ROWS, TQ), axis=0)


def _dsa_kernel(topk, iq_ref, wt_ref, k2_ref, q_ref, k_ref, vt_ref, tab_ref, o_ref,
                sc_ref, scb_ref, thr_ref, m_ref, acc_ref):
    qi = pl.program_id(1)
    n_tiles = qi + 1
    lane = lax.broadcasted_iota(jnp.int32, (TQ, 2 * IDX_DIM), 1)
    wt = wt_ref[0] * (IDX_HEADS ** -0.5 * IDX_DIM ** -0.5)
    n_chunks = (n_tiles + 1) // 2
    key_c = lax.broadcasted_iota(jnp.int32, (CHUNK, TQ), 0)
    qry_c = lax.broadcasted_iota(jnp.int32, (CHUNK, TQ), 1)

    def chunk_rows(c):
        return pl.ds(pl.multiple_of(c * CHUNK, CHUNK), CHUNK)

    def score_chunk(c, last, hi_lo):
        k2 = k2_ref[0, chunk_rows(c), :]
        acc = jnp.zeros((CHUNK, TQ), F32)
        for pair in range(IDX_HEADS // 2):
            qp = iq_ref[0, :, pair * 2 * IDX_DIM:(pair + 1) * 2 * IDX_DIM]
            for half in range(2):
                keep = lane < IDX_DIM if half == 0 else lane >= IDX_DIM
                qc = jnp.where(keep, qp, jnp.zeros_like(qp))
                rel = jnp.maximum(lax.dot_general(k2, qc, _NT, preferred_element_type=F32), 0.0)
                hh = 2 * pair + half
                acc = acc + wt[hh:hh + 1, :] * rel
        low = acc
        if last:
            causal = c * CHUNK + key_c <= qi * TQ + qry_c
            low = jnp.where(causal, acc, jnp.inf)
            acc = jnp.where(causal, acc, -jnp.inf)
        sc_ref[chunk_rows(c), :] = acc
        scb_ref[chunk_rows(c), :] = acc.astype(BF16)
        return jnp.maximum(hi_lo[0], _fold(acc, jnp.max)), jnp.minimum(hi_lo[1], _fold(low, jnp.min))

    hi_lo = (jnp.full((ACC_ROWS, TQ), -jnp.inf, F32), jnp.full((ACC_ROWS, TQ), jnp.inf, F32))
    hi_lo = lax.fori_loop(0, n_chunks - 1, lambda c, st: score_chunk(c, False, st), hi_lo)
    hi_lo = score_chunk(n_chunks - 1, True, hi_lo)
    row_max = jnp.max(hi_lo[0], axis=0, keepdims=True)
    row_min = jnp.min(hi_lo[1], axis=0, keepdims=True)

    def reduce_tiles(fn, init, reduce, combine):
        def body(c, carry):
            return combine(carry, _fold(fn(sc_ref[chunk_rows(c), :], c), reduce))
        return reduce(lax.fori_loop(0, n_chunks, body, init), axis=0, keepdims=True)

    def count(pred):
        return reduce_tiles(lambda blk, kt: jnp.where(pred(blk, kt), 1.0, 0.0),
                            jnp.zeros((ACC_ROWS, TQ), F32), jnp.sum, jnp.add)

    t_q = qi * TQ + lax.broadcasted_iota(jnp.int32, (1, TQ), 1)
    few = t_q < topk
    kf = float(topk)

    def count_b(mid_b):
        def body(c, carry):
            blk = scb_ref[chunk_rows(c), :]
            ind = jnp.where(blk >= mid_b, jnp.ones_like(blk), jnp.zeros_like(blk))
            parts = ind.reshape(CHUNK // ACC_ROWS, ACC_ROWS, TQ)
            for g in range(CHUNK // ACC_ROWS):
                carry = carry + parts[g]
            return carry
        acc = lax.fori_loop(0, n_chunks, body, jnp.zeros((ACC_ROWS, TQ), BF16))
        return jnp.sum(acc.astype(F32), axis=0, keepdims=True)

    def coarse(_, st):
        lo, hi = st
        mid_b = (lo + (hi - lo) * 0.5).astype(BF16)
        ge = count_b(mid_b) >= kf
        mid = mid_b.astype(F32)
        return jnp.where(ge, mid, lo), jnp.where(ge, hi, mid)

    lo_b = row_min.astype(BF16).astype(F32)
    hi_b = (row_max + (jnp.abs(row_max) * 2.0 ** -6 + 1e-30)).astype(BF16).astype(F32)
    lo_b, hi_b = lax.fori_loop(0, COARSE_ITERS, coarse, (lo_b, hi_b))
    lo0 = lo_b - (jnp.abs(lo_b) * 2.0 ** -8 + 1e-30)
    hi0 = hi_b + (jnp.abs(hi_b) * 2.0 ** -8 + 1e-30)

    def cond(st):
        return jnp.logical_and(st[0] < 400, jnp.min(st[4]) < 0.5)

    def body(st):
        it, lo, hi, thr, done, tie = st
        mid = lo + (hi - lo) * 0.5
        cnt = count(lambda blk, kt: blk >= mid)
        live = done < 0.5
        exact = jnp.logical_and(live, cnt == kf)
        stuck = jnp.logical_and(jnp.logical_and(live, cnt != kf), jnp.logical_or(mid <= lo, mid >= hi))
        thr = jnp.where(exact, mid, jnp.where(stuck, lo, thr))
        tie = jnp.where(stuck, 1.0, tie)
        done = jnp.where(jnp.logical_or(exact, stuck), 1.0, done)
        lo = jnp.where(cnt > kf, mid, lo)
        hi = jnp.where(cnt < kf, mid, hi)
        return it + 1, lo, hi, thr, done, tie

    init = (jnp.int32(0), lo0, hi0,
            jnp.full((1, TQ), LOWEST, F32), jnp.where(few, 1.0, 0.0), jnp.zeros((1, TQ), F32))
    _, _, _, thr, _, tie = lax.while_loop(cond, body, init)
    thr_ref[...] = jnp.broadcast_to(thr, thr_ref.shape)

    @pl.when(jnp.max(tie) > 0.5)
    def _():
        is_tie = tie > 0.5
        need = kf - count(lambda blk, kt: blk > thr)

        def kept(limit):
            return count(lambda blk, c: jnp.logical_and(blk == thr, c * CHUNK + key_c <= limit))

        def jbody(_, st):
            jlo, jhi = st
            jmid = (jlo + jhi) // 2
            ok = kept(jmid) >= need
            return jnp.where(ok, jlo, jmid), jnp.where(ok, jmid, jhi)

        jlo0 = jnp.full((1, TQ), -1, jnp.int32)
        jhi0 = jnp.zeros((1, TQ), jnp.int32) + (n_chunks * CHUNK - 1)
        _, cut = lax.fori_loop(0, 14, jbody, (jlo0, jhi0))

        def drop(c, carry):
            blk = sc_ref[chunk_rows(c), :]
            extra = jnp.logical_and(jnp.logical_and(is_tie, blk == thr), c * CHUNK + key_c > cut)
            sc_ref[chunk_rows(c), :] = jnp.where(extra, -jnp.inf, blk)
            return carry

        lax.fori_loop(0, n_chunks, drop, 0)

    _init_softmax(m_ref, acc_ref)

    def attend(c, carry):
        rows = chunk_rows(c)
        sel = sc_ref[rows, :] >= thr_ref[0:1, :]

        def logits(h):
            bias = _bias_block(tab_ref, h, (qi * TQ - c * CHUNK) // LANES, DSA_FIRST, DSA_TILES, CHUNK)
            return bias + lax.dot_general(k_ref[0, rows, _head_cols(h)], _head(q_ref, h), _NT,
                                          preferred_element_type=F32)

        def update(h, s):
            _softmax_step(jnp.where(sel, s, NEG), vt_ref[_head_cols(h), rows], m_ref, acc_ref, h)

        _pipelined(N_HEADS, logits, update)
        return carry

    lax.fori_loop(0, n_chunks, attend, 0)
    for h in range(N_HEADS):
        _store_head(o_ref, h, _normalized(acc_ref, h))


def _dsa_attention(pack, vt, w_t, k2, tab):
    bsz, seq, _ = pack.shape
    nq = seq // TQ
    topk = min(TOPK_MAX, seq // 4)
    qb = _q_block(MIX_A)
    return pl.pallas_call(
        functools.partial(_dsa_kernel, topk),
        grid=(bsz, nq),
        in_specs=[pl.BlockSpec((1, TQ, IDX_HEADS * IDX_DIM), lambda b, i: (b, i, 0)),
                  pl.BlockSpec((1, IDX_HEADS, TQ), lambda b, i: (b, 0, i)),
                  pl.BlockSpec((1, seq, 2 * IDX_DIM), lambda b, i: (b, 0, 0), pipeline_mode=_ONCE_PER_BATCH),
                  pl.BlockSpec((1, TQ, BRANCH_WIDTH), lambda b, i: (b, i, qb)),
                  pl.BlockSpec((1, seq, BRANCH_WIDTH), lambda b, i: (b, 0, qb + 1), pipeline_mode=_ONCE_PER_BATCH),
                  pl.BlockSpec((BRANCH_WIDTH, seq), lambda b, i: (MIX_A, b), pipeline_mode=_ONCE_PER_BATCH),
                  pl.BlockSpec(tab.shape, lambda b, i: (0, 0, 0, 0))],
        out_specs=pl.BlockSpec((1, TQ, BRANCH_WIDTH), lambda b, i: (b, i, 0)),
        out_shape=jax.ShapeDtypeStruct((bsz, seq, BRANCH_WIDTH), F32),
        scratch_shapes=[pltpu.VMEM((seq, TQ), F32), pltpu.VMEM((seq, TQ), BF16), pltpu.VMEM((SUBLANES, TQ), F32)]
        + _softmax_scratch(N_HEADS),
        compiler_params=_cparams(("parallel", "arbitrary")),
        name="dsa_attention",
    )(pack, w_t, k2, pack, pack, vt, tab)


def _merge_kernel(oa_ref, ob_ref, oc_ref, od_ref, z_ref, g0_ref, g1_ref, g2_ref, g3_ref, wb_ref, o_ref):
    merged = None
    for b, (o_ref_b, g_ref) in enumerate(zip((oa_ref, ob_ref, oc_ref, od_ref),
                                             (g0_ref, g1_ref, g2_ref, g3_ref))):
        z = z_ref[:, b * BRANCH_WIDTH:(b + 1) * BRANCH_WIDTH].astype(F32)
        y = o_ref_b[...] * (z * jax.nn.sigmoid(z))
        gate = jax.nn.sigmoid(g_ref[...].astype(F32))
        term = gate * jnp.dot(y.astype(BF16), wb_ref[b], preferred_element_type=F32)
        merged = term if merged is None else merged + term
    o_ref[...] = merged.astype(o_ref.dtype)


def _merge(branches, zg, w_branch, tm=256):
    m = zg.shape[0]
    o_spec = pl.BlockSpec((tm, BRANCH_WIDTH), lambda i: (i, 0))
    g_specs = [pl.BlockSpec((tm, D_MODEL), functools.partial(lambda i, b: (i, 1 + b), b=b))
               for b in range(N_BRANCHES)]
    return pl.pallas_call(
        _merge_kernel,
        grid=(m // tm,),
        in_specs=[o_spec] * N_BRANCHES + [pl.BlockSpec((tm, N_BRANCHES * BRANCH_WIDTH), lambda i: (i, 0))]
        + g_specs + [pl.BlockSpec(w_branch.shape, lambda i: (0, 0, 0))],
        out_specs=pl.BlockSpec((tm, D_MODEL), lambda i: (i, 0)),
        out_shape=jax.ShapeDtypeStruct((m, D_MODEL), BF16),
        compiler_params=_cparams(("parallel",)),
        name="gated_merge",
    )(*branches, zg, zg, zg, zg, zg, w_branch)


def _pack_w_in(w_in):
    sizes = (N_BRANCHES * BRANCH_WIDTH, N_BRANCHES * D_MODEL, 3 * BRANCH_WIDTH, IDX_HEADS * IDX_DIM,
             IDX_DIM, IDX_HEADS, 3 * BRANCH_WIDTH, N_HEADS, 3 * BRANCH_WIDTH, 3 * BRANCH_WIDTH)
    offs = np.concatenate([[0], np.cumsum(sizes)])
    w = w_in.astype(BF16)
    z, gate, a, iq, ik, iw, b, ff, c, d = [w[:, :, offs[n]:offs[n + 1]] for n in range(len(sizes))]
    qk = [m[:, :, :2 * BRANCH_WIDTH] for m in (a, b, c, d)]
    v = [m[:, :, 2 * BRANCH_WIDTH:] for m in (a, b, c, d)]
    pad = jnp.zeros(w.shape[:2] + (N_MISC - MISC_F - N_HEADS,), w.dtype)
    return (jnp.concatenate([iq] + qk, axis=-1), jnp.concatenate(v, axis=-1),
            jnp.concatenate([z, gate], axis=-1), jnp.concatenate([ik, ik, iw, ff, pad], axis=-1))


def _pack_col_scale():
    s = np.ones((1, N_PACK), np.float32)
    for mixer, sc in enumerate((HEAD_DIM ** -0.5, HEAD_DIM ** -0.5, DIFF_DIM ** -0.5, HEAD_DIM ** -0.5)):
        col = COL_QK + mixer * 2 * BRANCH_WIDTH
        s[:, col:col + BRANCH_WIDTH] = sc * LOG2E
    return jnp.asarray(s)


def _layer(x2, bsz, seq, layer, norm_w, w_pack, w_v, w_zg, w_misc, fox_b_f, lqk, subln_w, w_branch, w_out,
           tab_dsa, tab_diff, tab_dil):
    h = _rmsnorm(x2, norm_w, BF16)
    pack = _matmul(h, w_pack, BF16, PROJ_TM, PROJ_TN, col_scale=_pack_col_scale(),
                   name="proj_pack").reshape(bsz, seq, N_PACK)
    vt = _matmul(h, w_v, BF16, PROJ_TM, PROJ_TN, transpose_out=True, name="proj_vt")
    zg = _matmul(h, w_zg, BF16, PROJ_TM, PROJ_TN, name="proj_zg")
    misc = _matmul(h, w_misc, F32, PROJ_TM, N_MISC, name="proj_misc").reshape(bsz, seq, N_MISC)

    k2 = misc[:, :, :MISC_W].astype(BF16)
    w_t = jnp.transpose(misc[:, :, MISC_W:MISC_F], (0, 2, 1))
    o_dsa = _dsa_attention(pack, vt, w_t, k2, tab_dsa)

    f_bsh = misc[:, :, MISC_F:MISC_F + N_HEADS]
    cum_bhs = _fox_cum(jnp.transpose(f_bsh, (0, 2, 1)), fox_b_f)
    o_fox = _fox_attention(pack, vt, jnp.transpose(cum_bhs, (0, 2, 1)), cum_bhs)

    lambda_init = 0.8 - 0.6 * math.exp(-0.3 * layer)
    o_diff = _diff_attention(pack, vt, tab_diff, lqk, subln_w, lambda_init)
    o_dil = _dil_attention(pack, vt, tab_dil)

    m = bsz * seq
    branches = [o.reshape(m, BRANCH_WIDTH) for o in (o_dsa, o_fox, o_diff, o_dil)]
    merged = _merge(branches, zg, w_branch)
    return _matmul(merged, w_out, F32, PROJ_TM, PROJ_TN, residual=x2, name="out_proj")


def kernel(x, norm_w, w_in, fox_b_f, diff_lq1, diff_lk1, diff_lq2, diff_lk2, diff_subln_w, w_branch, w_out,
           rel_bias, final_norm_w):
    bsz, seq, d = x.shape
    w_pack, w_v, w_zg, w_misc = _pack_w_in(w_in)
    w_branch_bf = w_branch.astype(BF16)
    w_out_bf = w_out.astype(BF16)
    lqk = jnp.stack([diff_lq1, diff_lk1, diff_lq2, diff_lk2], axis=1)
    bias_dsa, bias_diff, bias_dil = jnp.split(rel_bias, 3, axis=-1)
    tab_dsa = _bias_tiles(bias_dsa, DSA_FIRST, DSA_TILES, causal=False)
    tab_diff = _bias_tiles(bias_diff, DIFF_FIRST, DIFF_TILES, causal=True)
    tab_dil = _dilated_tiles(bias_dil, DIL_FIRST, DIL_TILES)

    x2 = x.reshape(bsz * seq, d)
    for layer in range(DEPTH):
        x2 = _layer(x2, bsz, seq, layer, norm_w[layer], w_pack[layer], w_v[layer], w_zg[layer], w_misc[layer],
                    fox_b_f[layer], lqk[layer], diff_subln_w[layer], w_branch_bf[layer], w_out_bf[layer],
                    tab_dsa, tab_diff, tab_dil)
    return _rmsnorm(x2, final_norm_w, F32).reshape(bsz, seq, d)
```

```python
import functools
import math

import jax
import jax.numpy as jnp
import numpy as np
from jax import lax
from jax.experimental import pallas as pl
from jax.experimental.pallas import tpu as pltpu

F32 = jnp.float32
BF16 = jnp.bfloat16

D_MODEL = 2048
DEPTH = 4
HEAD_DIM = 128
N_HEADS = 4
BRANCH_WIDTH = N_HEADS * HEAD_DIM
N_BRANCHES = 4
DIFF_DIM = HEAD_DIM // 2
IDX_HEADS = 16
IDX_DIM = 64
TOPK_MAX = 256
DILATED_CONFIGS = ((128, 1), (512, 4), (2048, 16))
N_BUCKETS = 32
MAX_DISTANCE = 2048
RMS_EPS = 1e-6

LANES = 128
SUBLANES = 8
ONES_ROWS = 16
TQ = 256
TK = 256
TKF = 512
LOG2E = math.log2(math.e)
PROJ_TM, PROJ_TN = 2048, 512
NEG = -1e30
VMEM_LIMIT = 56 * 1024 * 1024

COL_IDXQ = 0
COL_QK = IDX_HEADS * IDX_DIM
N_PACK = COL_QK + N_BRANCHES * 2 * BRANCH_WIDTH
N_V = N_BRANCHES * BRANCH_WIDTH
N_ZG = N_BRANCHES * BRANCH_WIDTH + N_BRANCHES * D_MODEL
N_MISC = 256
MISC_W = 2 * IDX_DIM
MISC_F = MISC_W + IDX_HEADS
MIX_A, MIX_B, MIX_C, MIX_D = range(N_BRANCHES)

_NT = (((1,), (1,)), ((), ()))
_ONCE_PER_BATCH = pl.Buffered(1)


def _cparams(sem):
    return pltpu.CompilerParams(dimension_semantics=sem, vmem_limit_bytes=VMEM_LIMIT)


def _q_block(mixer):
    return COL_QK // BRANCH_WIDTH + 2 * mixer


def _rmsnorm_kernel(x_ref, w_ref, o_ref):
    x = x_ref[...]
    y = x * lax.rsqrt(jnp.mean(x * x, axis=-1, keepdims=True) + RMS_EPS)
    o_ref[...] = (y * w_ref[...]).astype(o_ref.dtype)


def _rmsnorm(x2, w, out_dtype, tm=512):
    m, d = x2.shape
    return pl.pallas_call(
        _rmsnorm_kernel,
        grid=(m // tm,),
        in_specs=[pl.BlockSpec((tm, d), lambda i: (i, 0)), pl.BlockSpec((1, d), lambda i: (0, 0))],
        out_specs=pl.BlockSpec((tm, d), lambda i: (i, 0)),
        out_shape=jax.ShapeDtypeStruct((m, d), out_dtype),
        compiler_params=_cparams(("parallel",)),
        name="rmsnorm",
    )(x2, w.reshape(1, d))


def _matmul_kernel(a_ref, b_ref, o_ref):
    o_ref[...] = jnp.dot(a_ref[...], b_ref[...], preferred_element_type=F32).astype(o_ref.dtype)


def _matmul_t_kernel(a_ref, b_ref, o_ref):
    o_ref[...] = jnp.dot(a_ref[...], b_ref[...], preferred_element_type=F32).T.astype(o_ref.dtype)


def _matmul_res_kernel(a_ref, b_ref, r_ref, o_ref):
    o_ref[...] = r_ref[...] + jnp.dot(a_ref[...], b_ref[...], preferred_element_type=F32)


def _matmul_scale_kernel(a_ref, b_ref, s_ref, o_ref):
    o_ref[...] = (jnp.dot(a_ref[...], b_ref[...], preferred_element_type=F32) * s_ref[...]).astype(o_ref.dtype)


def _matmul(a, b, out_dtype, tm, tn, residual=None, col_scale=None, transpose_out=False, name="matmul"):
    m, k = a.shape
    n = b.shape[1]
    tm, tn = min(tm, m), min(tn, n)
    in_specs = [pl.BlockSpec((tm, k), lambda i, j: (i, 0)), pl.BlockSpec((k, tn), lambda i, j: (0, j))]
    args = [a, b]
    kern = _matmul_kernel
    out_spec = pl.BlockSpec((tm, tn), lambda i, j: (i, j))
    out_shape = (m, n)
    if residual is not None:
        in_specs.append(pl.BlockSpec((tm, tn), lambda i, j: (i, j)))
        args.append(residual)
        kern = _matmul_res_kernel
    if col_scale is not None:
        in_specs.append(pl.BlockSpec((1, tn), lambda i, j: (0, j)))
        args.append(col_scale)
        kern = _matmul_scale_kernel
    if transpose_out:
        kern = _matmul_t_kernel
        out_spec = pl.BlockSpec((tn, tm), lambda i, j: (j, i))
        out_shape = (n, m)
    return pl.pallas_call(
        kern,
        grid=(m // tm, n // tn),
        in_specs=in_specs,
        out_specs=out_spec,
        out_shape=jax.ShapeDtypeStruct(out_shape, out_dtype),
        compiler_params=_cparams(("parallel", "arbitrary")),
        name=name,
    )(*args)


def _t5_bucket(dist):
    dist = jnp.maximum(dist, 0)
    max_exact = N_BUCKETS // 2
    d = jnp.maximum(dist, max_exact).astype(F32)
    large = max_exact + (jnp.log(d / max_exact) / math.log(MAX_DISTANCE / max_exact)
                         * (N_BUCKETS - max_exact)).astype(jnp.int32)
    large = jnp.minimum(large, N_BUCKETS - 1)
    return jnp.where(dist < max_exact, dist, large)


def _toeplitz_tiles(vals, n_tiles):
    nh, length = vals.shape
    hank = jnp.tile(vals, (1, LANES + 1))[:, :LANES * (length + 1)].reshape(nh, LANES, length + 1)
    hank = hank[:, ::-1, :n_tiles * LANES]
    return jnp.transpose(hank.reshape(nh, LANES, n_tiles, LANES), (0, 2, 1, 3))


def _distances(first, n_tiles):
    return first * LANES - (LANES - 1) + jnp.arange(n_tiles * LANES + LANES - 1)


def _bias_tiles(table, first, n_tiles, causal):
    d = _distances(first, n_tiles)
    vals = table[_t5_bucket(d)].T
    if causal:
        vals = jnp.where(d[None] >= 0, vals, NEG)
    return _toeplitz_tiles(vals * LOG2E, n_tiles)


def _dilated_tiles(table, first, n_tiles):
    d = _distances(first, n_tiles)
    mult = jnp.zeros(d.shape, F32)
    for window, dilation in DILATED_CONFIGS:
        mult = mult + ((d >= 0) & (d <= window) & (d % dilation == 0)).astype(F32)
    vals = table[_t5_bucket(d)].T + jnp.log(jnp.maximum(mult, 1.0))[None]
    return _toeplitz_tiles(jnp.where(mult[None] > 0, vals, NEG) * LOG2E, n_tiles)


def _bias_block(tab_ref, h, a, first, n_tiles, tk=TK):
    def tile(off):
        return tab_ref[h, jnp.clip(off - first, 0, n_tiles - 1)]
    rows = [jnp.concatenate([tile(a + ib - jb) for ib in range(TQ // LANES)], axis=1)
            for jb in range(tk // LANES)]
    return jnp.concatenate(rows, axis=0)


def _softmax_step(s, vt, m_ref, acc_ref, idx):
    m_prev = m_ref[idx]
    m_next = jnp.maximum(m_prev, jnp.max(s, axis=0, keepdims=True))
    alpha = jnp.exp2(m_prev - m_next)
    p = jnp.exp2(s - m_next).astype(BF16)
    vt_ones = jnp.concatenate([vt, jnp.ones((ONES_ROWS, vt.shape[1]), vt.dtype)], axis=0)
    m_ref[idx] = m_next
    acc_ref[idx] = acc_ref[idx] * alpha + jnp.dot(vt_ones, p, preferred_element_type=F32)


def _normalized(acc_ref, idx):
    acc = acc_ref[idx]
    return acc[:HEAD_DIM] / acc[HEAD_DIM:HEAD_DIM + 1]


def _pipelined(n, logits_fn, update_fn, lookahead=3):
    pending = {i: logits_fn(i) for i in range(min(lookahead, n))}
    for i in range(n):
        if i + lookahead < n:
            pending[i + lookahead] = logits_fn(i + lookahead)
        update_fn(i, pending.pop(i))


def _init_softmax(m_ref, acc_ref):
    m_ref[...] = jnp.full(m_ref.shape, NEG, F32)
    acc_ref[...] = jnp.zeros(acc_ref.shape, F32)


def _softmax_scratch(n):
    return [pltpu.VMEM((n, 1, TQ), F32), pltpu.VMEM((n, HEAD_DIM + ONES_ROWS, TQ), F32)]


def _head(ref, h):
    return ref[0, :, h * HEAD_DIM:(h + 1) * HEAD_DIM]


def _store_head(o_ref, h, o_t):
    o_ref[0, :, h * HEAD_DIM:(h + 1) * HEAD_DIM] = o_t.T


def _logsig_cumsum_kernel(f_ref, b_ref, o_ref):
    x = f_ref[0, 0] + b_ref[0, 0]
    ls = jnp.minimum(x, 0.0) - jnp.log1p(jnp.exp(-jnp.abs(x)))
    rows = ls.shape[0]
    r = lax.broadcasted_iota(jnp.int32, (LANES, LANES), 0)
    c = lax.broadcasted_iota(jnp.int32, (LANES, LANES), 1)
    upper = (r <= c).astype(F32)
    within = jnp.dot(ls, upper, preferred_element_type=F32, precision=lax.Precision.HIGHEST)
    total = jnp.broadcast_to(within[:, LANES - 1:LANES], (rows, LANES))
    rr = lax.broadcasted_iota(jnp.int32, (rows, rows), 0)
    rc = lax.broadcasted_iota(jnp.int32, (rows, rows), 1)
    strict = (rc < rr).astype(F32)
    before = jnp.dot(strict, total, preferred_element_type=F32, precision=lax.Precision.HIGHEST)
    o_ref[0, 0] = within + before


def _fox_cum(f_bhs, b_f):
    bsz, nh, seq = f_bhs.shape
    rows = seq // LANES
    out = pl.pallas_call(
        _logsig_cumsum_kernel,
        grid=(bsz, nh),
        in_specs=[pl.BlockSpec((1, 1, rows, LANES), lambda b, h: (b, h, 0, 0)),
                  pl.BlockSpec((1, 1, 1, LANES), lambda b, h: (0, h, 0, 0))],
        out_specs=pl.BlockSpec((1, 1, rows, LANES), lambda b, h: (b, h, 0, 0)),
        out_shape=jax.ShapeDtypeStruct((bsz, nh, rows, LANES), F32),
        compiler_params=_cparams(("parallel", "parallel")),
        name="fox_cumsum",
    )(f_bhs.reshape(bsz, nh, rows, LANES), jnp.broadcast_to(b_f[None, :, None, None], (1, nh, 1, LANES)))
    return out.reshape(bsz, nh, seq)


def _last_key_tile(qi, tk):
    return (qi * TQ + TQ - 1) // tk


def _key_rows(c, tk):
    return pl.ds(pl.multiple_of(c * tk, tk), tk)


def _head_cols(h):
    return slice(h * HEAD_DIM, (h + 1) * HEAD_DIM)


def _seq_specs(mixer, seq):
    qb = _q_block(mixer)
    return [pl.BlockSpec((1, TQ, BRANCH_WIDTH), lambda b, i: (b, i, qb)),
            pl.BlockSpec((1, seq, BRANCH_WIDTH), lambda b, i: (b, 0, qb + 1), pipeline_mode=_ONCE_PER_BATCH),
            pl.BlockSpec((BRANCH_WIDTH, seq), lambda b, i: (mixer, b), pipeline_mode=_ONCE_PER_BATCH)]


def _fox_kernel(q_ref, k_ref, vt_ref, cq_ref, ck_ref, o_ref, m_ref, acc_ref):
    qi = pl.program_id(1)
    last = _last_key_tile(qi, TKF)
    _init_softmax(m_ref, acc_ref)

    def chunk(c, diagonal):
        rows = _key_rows(c, TKF)
        if diagonal:
            key = c * TKF + lax.broadcasted_iota(jnp.int32, (TKF, TQ), 0)
            qry = qi * TQ + lax.broadcasted_iota(jnp.int32, (TKF, TQ), 1)
            causal = key <= qry

        def logits(h):
            decay = cq_ref[0, h:h + 1, :] * LOG2E - ck_ref[0, rows, h:h + 1] * LOG2E
            return decay + lax.dot_general(k_ref[0, rows, _head_cols(h)], _head(q_ref, h), _NT,
                                           preferred_element_type=F32)

        def update(h, s):
            if diagonal:
                s = jnp.where(causal, s, NEG)
            _softmax_step(s, vt_ref[_head_cols(h), rows], m_ref, acc_ref, h)

        _pipelined(N_HEADS, logits, update)

    def body(c, carry):
        chunk(c, False)
        return carry

    lax.fori_loop(0, last, body, 0)
    chunk(last, True)
    for h in range(N_HEADS):
        _store_head(o_ref, h, _normalized(acc_ref, h))


def _fox_attention(pack, vt, cum_bsh, cum_bhs):
    bsz, seq, _ = pack.shape
    return pl.pallas_call(
        _fox_kernel,
        grid=(bsz, seq // TQ),
        in_specs=_seq_specs(MIX_B, seq)
        + [pl.BlockSpec((1, N_HEADS, TQ), lambda b, i: (b, 0, i)),
           pl.BlockSpec((1, seq, N_HEADS), lambda b, i: (b, 0, 0))],
        out_specs=pl.BlockSpec((1, TQ, BRANCH_WIDTH), lambda b, i: (b, i, 0)),
        out_shape=jax.ShapeDtypeStruct((bsz, seq, BRANCH_WIDTH), F32),
        scratch_shapes=_softmax_scratch(N_HEADS),
        compiler_params=_cparams(("parallel", "arbitrary")),
        name="fox_attention",
    )(pack, pack, vt, cum_bhs, cum_bsh)


DIFF_FIRST, DIFF_TILES = -1, 15


def _diff_kernel(lambda_init, q_ref, k_ref, vt_ref, tab_ref, lqk_ref, sub_ref, o_ref, m_ref, acc_ref):
    qi = pl.program_id(1)
    _init_softmax(m_ref, acc_ref)
    lane = lax.broadcasted_iota(jnp.int32, (TQ, HEAD_DIM), 1)

    def body(c, carry):
        rows = _key_rows(c, TKF)

        def logits(i):
            h, m = divmod(i, 2)
            q = _head(q_ref, h)
            keep = lane < DIFF_DIM if m == 0 else lane >= DIFF_DIM
            qc = jnp.where(keep, q, jnp.zeros_like(q))
            bias = _bias_block(tab_ref, h, (qi * TQ - c * TKF) // LANES, DIFF_FIRST, DIFF_TILES, TKF)
            return bias + lax.dot_general(k_ref[0, rows, _head_cols(h)], qc, _NT, preferred_element_type=F32)

        def update(i, s):
            _softmax_step(s, vt_ref[_head_cols(i // 2), rows], m_ref, acc_ref, i)

        _pipelined(2 * N_HEADS, logits, update)
        return carry

    lax.fori_loop(0, _last_key_tile(qi, TKF) + 1, body, 0)

    lqk = lqk_ref[...]
    lam = (jnp.exp(jnp.sum(lqk[0:1] * lqk[1:2], axis=1, keepdims=True))
           - jnp.exp(jnp.sum(lqk[2:3] * lqk[3:4], axis=1, keepdims=True)) + lambda_init)
    for h in range(N_HEADS):
        o = _normalized(acc_ref, 2 * h) - lam * _normalized(acc_ref, 2 * h + 1)
        y = o * lax.rsqrt(jnp.mean(o * o, axis=0, keepdims=True) + RMS_EPS) * sub_ref[...]
        _store_head(o_ref, h, y * (1.0 - lambda_init))


def _diff_attention(pack, vt, tab, lqk, subln_w, lambda_init):
    bsz, seq, _ = pack.shape
    sub = jnp.broadcast_to(subln_w[:, None], (HEAD_DIM, TQ))
    return pl.pallas_call(
        functools.partial(_diff_kernel, lambda_init),
        grid=(bsz, seq // TQ),
        in_specs=_seq_specs(MIX_C, seq)
        + [pl.BlockSpec(tab.shape, lambda b, i: (0, 0, 0, 0)),
           pl.BlockSpec(lqk.shape, lambda b, i: (0, 0)),
           pl.BlockSpec((HEAD_DIM, TQ), lambda b, i: (0, 0))],
        out_specs=pl.BlockSpec((1, TQ, BRANCH_WIDTH), lambda b, i: (b, i, 0)),
        out_shape=jax.ShapeDtypeStruct((bsz, seq, BRANCH_WIDTH), F32),
        scratch_shapes=_softmax_scratch(2 * N_HEADS),
        compiler_params=_cparams(("parallel", "arbitrary")),
        name="diff_attention",
    )(pack, pack, vt, tab, lqk, sub)


DIL_FIRST = -1
DIL_TILES = MAX_DISTANCE // LANES + 1 - DIL_FIRST + 1


def _dil_kernel(q_ref, k_ref, vt_ref, tab_ref, o_ref, m_ref, acc_ref):
    qi = pl.program_id(1)
    _init_softmax(m_ref, acc_ref)

    def body(c, carry):
        rows = _key_rows(c, TKF)

        def logits(h):
            bias = _bias_block(tab_ref, h, (qi * TQ - c * TKF) // LANES, DIL_FIRST, DIL_TILES, TKF)
            return bias + lax.dot_general(k_ref[0, rows, _head_cols(h)], _head(q_ref, h), _NT,
                                          preferred_element_type=F32)

        def update(h, s):
            _softmax_step(s, vt_ref[_head_cols(h), rows], m_ref, acc_ref, h)

        _pipelined(N_HEADS, logits, update)
        return carry

    first = jnp.maximum(qi * TQ - MAX_DISTANCE, 0) // TKF
    lax.fori_loop(first, _last_key_tile(qi, TKF) + 1, body, 0)
    for h in range(N_HEADS):
        _store_head(o_ref, h, _normalized(acc_ref, h))


def _dil_attention(pack, vt, tab):
    bsz, seq, _ = pack.shape
    return pl.pallas_call(
        _dil_kernel,
        grid=(bsz, seq // TQ),
        in_specs=_seq_specs(MIX_D, seq) + [pl.BlockSpec(tab.shape, lambda b, i: (0, 0, 0, 0))],
        out_specs=pl.BlockSpec((1, TQ, BRANCH_WIDTH), lambda b, i: (b, i, 0)),
        out_shape=jax.ShapeDtypeStruct((bsz, seq, BRANCH_WIDTH), F32),
        scratch_shapes=_softmax_scratch(N_HEADS),
        compiler_params=_cparams(("parallel", "arbitrary")),
        name="dilated_attention",
    )(pack, pack, vt, tab)


DSA_FIRST, DSA_TILES = -1, 15
LOWEST = -3.0e38
CHUNK = 2 * TK
ACC_ROWS = 4 * SUBLANES
COARSE_ITERS = 10


def _fold(x, op):
    return op(x.reshape(CHUNK // ACC_ROWS, ACC_ROWS, TQ), axis=0)


def _dsa_kernel(topk, iq_ref, wt_ref, k2_ref, q_ref, k_ref, vt_ref, tab_ref, o_ref,
                sc_ref, scb_ref, thr_ref, m_ref, acc_ref):
    qi = pl.program_id(1)
    n_tiles = qi + 1
    lane = lax.broadcasted_iota(jnp.int32, (TQ, 2 * IDX_DIM), 1)
    wt = wt_ref[0] * (IDX_HEADS ** -0.5 * IDX_DIM ** -0.5)
    n_chunks = (n_tiles + 1) // 2
    key_c = lax.broadcasted_iota(jnp.int32, (CHUNK, TQ), 0)
    qry_c = lax.broadcasted_iota(jnp.int32, (CHUNK, TQ), 1)

    def chunk_rows(c):
        return pl.ds(pl.multiple_of(c * CHUNK, CHUNK), CHUNK)

    def score_chunk(c, last, hi_lo):
        k2 = k2_ref[0, chunk_rows(c), :]
        acc = jnp.zeros((CHUNK, TQ), F32)
        for pair in range(IDX_HEADS // 2):
            qp = iq_ref[0, :, pair * 2 * IDX_DIM:(pair + 1) * 2 * IDX_DIM]
            for half in range(2):
                keep = lane < IDX_DIM if half == 0 else lane >= IDX_DIM
                qc = jnp.where(keep, qp, jnp.zeros_like(qp))
                rel = jnp.maximum(lax.dot_general(k2, qc, _NT, preferred_element_type=F32), 0.0)
                hh = 2 * pair + half
                acc = acc + wt[hh:hh + 1, :] * rel
        low = acc
        if last:
            causal = c * CHUNK + key_c <= qi * TQ + qry_c
            low = jnp.where(causal, acc, jnp.inf)
            acc = jnp.where(causal, acc, -jnp.inf)
        sc_ref[chunk_rows(c), :] = acc
        scb_ref[chunk_rows(c), :] = acc.astype(BF16)
        return jnp.maximum(hi_lo[0], _fold(acc, jnp.max)), jnp.minimum(hi_lo[1], _fold(low, jnp.min))

    hi_lo = (jnp.full((ACC_ROWS, TQ), -jnp.inf, F32), jnp.full((ACC_ROWS, TQ), jnp.inf, F32))
    hi_lo = lax.fori_loop(0, n_chunks - 1, lambda c, st: score_chunk(c, False, st), hi_lo)
    hi_lo = score_chunk(n_chunks - 1, True, hi_lo)
    row_max = jnp.max(hi_lo[0], axis=0, keepdims=True)
    row_min = jnp.min(hi_lo[1], axis=0, keepdims=True)

    def reduce_tiles(fn, init, reduce, combine):
        def body(c, carry):
            return combine(carry, _fold(fn(sc_ref[chunk_rows(c), :], c), reduce))
        return reduce(lax.fori_loop(0, n_chunks, body, init), axis=0, keepdims=True)

    def count(pred):
        return reduce_tiles(lambda blk, kt: jnp.where(pred(blk, kt), 1.0, 0.0),
                            jnp.zeros((ACC_ROWS, TQ), F32), jnp.sum, jnp.add)

    t_q = qi * TQ + lax.broadcasted_iota(jnp.int32, (1, TQ), 1)
    few = t_q < topk
    kf = float(topk)

    def count_b(mid_b):
        def body(c, carry):
            blk = scb_ref[chunk_rows(c), :]
            ind = jnp.where(blk >= mid_b, jnp.ones_like(blk), jnp.zeros_like(blk))
            parts = ind.reshape(CHUNK // ACC_ROWS, ACC_ROWS, TQ)
            for g in range(CHUNK // ACC_ROWS):
                carry = carry + parts[g]
            return carry
        acc = lax.fori_loop(0, n_chunks, body, jnp.zeros((ACC_ROWS, TQ), BF16))
        return jnp.sum(acc.astype(F32), axis=0, keepdims=True)

    def coarse(_, st):
        lo, hi = st
        mid_b = (lo + (hi - lo) * 0.5).astype(BF16)
        ge = count_b(mid_b) >= kf
        mid = mid_b.astype(F32)
        return jnp.where(ge, mid, lo), jnp.where(ge, hi, mid)

    lo_b = row_min.astype(BF16).astype(F32)
    hi_b = (row_max + (jnp.abs(row_max) * 2.0 ** -6 + 1e-30)).astype(BF16).astype(F32)
    lo_b, hi_b = lax.fori_loop(0, COARSE_ITERS, coarse, (lo_b, hi_b))
    lo0 = lo_b - (jnp.abs(lo_b) * 2.0 ** -8 + 1e-30)
    hi0 = hi_b + (jnp.abs(hi_b) * 2.0 ** -8 + 1e-30)

    def cond(st):
        return jnp.logical_and(st[0] < 400, jnp.min(st[4]) < 0.5)

    def body(st):
        it, lo, hi, thr, done, tie = st
        mid = lo + (hi - lo) * 0.5
        cnt = count(lambda blk, kt: blk >= mid)
        live = done < 0.5
        exact = jnp.logical_and(live, cnt == kf)
        stuck = jnp.logical_and(jnp.logical_and(live, cnt != kf), jnp.logical_or(mid <= lo, mid >= hi))
        thr = jnp.where(exact, mid, jnp.where(stuck, lo, thr))
        tie = jnp.where(stuck, 1.0, tie)
        done = jnp.where(jnp.logical_or(exact, stuck), 1.0, done)
        lo = jnp.where(cnt > kf, mid, lo)
        hi = jnp.where(cnt < kf, mid, hi)
        return it + 1, lo, hi, thr, done, tie

    init = (jnp.int32(0), lo0, hi0,
            jnp.full((1, TQ), LOWEST, F32), jnp.where(few, 1.0, 0.0), jnp.zeros((1, TQ), F32))
    _, _, _, thr, _, tie = lax.while_loop(cond, body, init)
    thr_ref[...] = jnp.broadcast_to(thr, thr_ref.shape)

    @pl.when(jnp.max(tie) > 0.5)
    def _():
        is_tie = tie > 0.5
        need = kf - count(lambda blk, kt: blk > thr)

        def kept(limit):
            return count(lambda blk, c: jnp.logical_and(blk == thr, c * CHUNK + key_c <= limit))

        def jbody(_, st):
            jlo, jhi = st
            jmid = (jlo + jhi) // 2
            ok = kept(jmid) >= need
            return jnp.where(ok, jlo, jmid), jnp.where(ok, jmid, jhi)

        jlo0 = jnp.full((1, TQ), -1, jnp.int32)
        jhi0 = jnp.zeros((1, TQ), jnp.int32) + (n_chunks * CHUNK - 1)
        _, cut = lax.fori_loop(0, 14, jbody, (jlo0, jhi0))

        def drop(c, carry):
            blk = sc_ref[chunk_rows(c), :]
            extra = jnp.logical_and(jnp.logical_and(is_tie, blk == thr), c * CHUNK + key_c > cut)
            sc_ref[chunk_rows(c), :] = jnp.where(extra, -jnp.inf, blk)
            return carry

        lax.fori_loop(0, n_chunks, drop, 0)

    _init_softmax(m_ref, acc_ref)

    def attend(c, carry):
        rows = chunk_rows(c)
        sel = sc_ref[rows, :] >= thr_ref[0:1, :]

        def logits(h):
            bias = _bias_block(tab_ref, h, (qi * TQ - c * CHUNK) // LANES, DSA_FIRST, DSA_TILES, CHUNK)
            return bias + lax.dot_general(k_ref[0, rows, _head_cols(h)], _head(q_ref, h), _NT,
                                          preferred_element_type=F32)

        def update(h, s):
            _softmax_step(jnp.where(sel, s, NEG), vt_ref[_head_cols(h), rows], m_ref, acc_ref, h)

        _pipelined(N_HEADS, logits, update)
        return carry

    lax.fori_loop(0, n_chunks, attend, 0)
    for h in range(N_HEADS):
        _store_head(o_ref, h, _normalized(acc_ref, h))


def _dsa_attention(pack, vt, w_t, k2, tab):
    bsz, seq, _ = pack.shape
    nq = seq // TQ
    topk = min(TOPK_MAX, seq // 4)
    qb = _q_block(MIX_A)
    return pl.pallas_call(
        functools.partial(_dsa_kernel, topk),
        grid=(bsz, nq),
        in_specs=[pl.BlockSpec((1, TQ, IDX_HEADS * IDX_DIM), lambda b, i: (b, i, 0)),
                  pl.BlockSpec((1, IDX_HEADS, TQ), lambda b, i: (b, 0, i)),
                  pl.BlockSpec((1, seq, 2 * IDX_DIM), lambda b, i: (b, 0, 0), pipeline_mode=_ONCE_PER_BATCH),
                  pl.BlockSpec((1, TQ, BRANCH_WIDTH), lambda b, i: (b, i, qb)),
                  pl.BlockSpec((1, seq, BRANCH_WIDTH), lambda b, i: (b, 0, qb + 1), pipeline_mode=_ONCE_PER_BATCH),
                  pl.BlockSpec((BRANCH_WIDTH, seq), lambda b, i: (MIX_A, b), pipeline_mode=_ONCE_PER_BATCH),
                  pl.BlockSpec(tab.shape, lambda b, i: (0, 0, 0, 0))],
        out_specs=pl.BlockSpec((1, TQ, BRANCH_WIDTH), lambda b, i: (b, i, 0)),
        out_shape=jax.ShapeDtypeStruct((bsz, seq, BRANCH_WIDTH), F32),
        scratch_shapes=[pltpu.VMEM((seq, TQ), F32), pltpu.VMEM((seq, TQ), BF16), pltpu.VMEM((SUBLANES, TQ), F32)]
        + _softmax_scratch(N_HEADS),
        compiler_params=_cparams(("parallel", "arbitrary")),
        name="dsa_attention",
    )(pack, w_t, k2, pack, pack, vt, tab)


def _merge_kernel(oa_ref, ob_ref, oc_ref, od_ref, z_ref, g0_ref, g1_ref, g2_ref, g3_ref, wb_ref, o_ref):
    merged = None
    for b, (o_ref_b, g_ref) in enumerate(zip((oa_ref, ob_ref, oc_ref, od_ref),
                                             (g0_ref, g1_ref, g2_ref, g3_ref))):
        zh = z_ref[:, b * BRANCH_WIDTH:(b + 1) * BRANCH_WIDTH].astype(F32)
        y = o_ref_b[...] * (zh * jnp.tanh(zh) + zh)
        half = jnp.dot(y.astype(BF16), wb_ref[b], preferred_element_type=F32)
        term = jnp.tanh(g_ref[...].astype(F32)) * half + half
        merged = term if merged is None else merged + term
    o_ref[...] = merged.astype(o_ref.dtype)


def _merge(branches, zg, w_branch, tm=256):
    m = zg.shape[0]
    o_spec = pl.BlockSpec((tm, BRANCH_WIDTH), lambda i: (i, 0))
    g_specs = [pl.BlockSpec((tm, D_MODEL), functools.partial(lambda i, b: (i, 1 + b), b=b))
               for b in range(N_BRANCHES)]
    return pl.pallas_call(
        _merge_kernel,
        grid=(m // tm,),
        in_specs=[o_spec] * N_BRANCHES + [pl.BlockSpec((tm, N_BRANCHES * BRANCH_WIDTH), lambda i: (i, 0))]
        + g_specs + [pl.BlockSpec(w_branch.shape, lambda i: (0, 0, 0))],
        out_specs=pl.BlockSpec((tm, D_MODEL), lambda i: (i, 0)),
        out_shape=jax.ShapeDtypeStruct((m, D_MODEL), BF16),
        compiler_params=_cparams(("parallel",)),
        name="gated_merge",
    )(*branches, zg, zg, zg, zg, zg, w_branch)


_IN_SIZES = (N_BRANCHES * BRANCH_WIDTH, N_BRANCHES * D_MODEL, 3 * BRANCH_WIDTH, IDX_HEADS * IDX_DIM,
             IDX_DIM, IDX_HEADS, 3 * BRANCH_WIDTH, N_HEADS, 3 * BRANCH_WIDTH, 3 * BRANCH_WIDTH)
_IN_OFFS = tuple(int(o) for o in np.concatenate([[0], np.cumsum(_IN_SIZES)]))
N_IN = _IN_OFFS[-1]
(G_Z, G_GATE, G_A, G_IQ, G_IK, G_IW, G_B, G_F, G_C, G_D) = range(len(_IN_SIZES))
REPACK_ROWS = 128


def _repack_kernel(w_ref, pack_ref, v_ref, zg_ref, misc_ref):
    def cols(group, start=0, width=None):
        lo = _IN_OFFS[group] + start
        hi = _IN_OFFS[group + 1] if width is None else lo + width
        return w_ref[0, :, lo:hi]

    def put(ref, col, value):
        ref[0, :, col:col + value.shape[1]] = value.astype(ref.dtype)

    put(pack_ref, COL_IDXQ, cols(G_IQ))
    for mixer, group in enumerate((G_A, G_B, G_C, G_D)):
        put(pack_ref, COL_QK + mixer * 2 * BRANCH_WIDTH, cols(group, 0, 2 * BRANCH_WIDTH))
        put(v_ref, mixer * BRANCH_WIDTH, cols(group, 2 * BRANCH_WIDTH, BRANCH_WIDTH))
    put(zg_ref, 0, cols(G_Z, 0, N_ZG) * 0.5)
    misc_ref[...] = jnp.zeros(misc_ref.shape, misc_ref.dtype)
    put(misc_ref, 0, cols(G_IK))
    put(misc_ref, IDX_DIM, cols(G_IK))
    put(misc_ref, MISC_W, cols(G_IW))
    put(misc_ref, MISC_F, cols(G_F))


def _pack_w_in(w_in):
    depth, d, n_in = w_in.shape
    widths = (N_PACK, N_V, N_ZG, N_MISC)
    return pl.pallas_call(
        _repack_kernel,
        grid=(depth, d // REPACK_ROWS),
        in_specs=[pl.BlockSpec((1, REPACK_ROWS, n_in), lambda l, i: (l, i, 0))],
        out_specs=[pl.BlockSpec((1, REPACK_ROWS, n), lambda l, i: (l, i, 0)) for n in widths],
        out_shape=[jax.ShapeDtypeStruct((depth, d, n), BF16) for n in widths],
        compiler_params=_cparams(("parallel", "parallel")),
        name="repack_w_in",
    )(w_in)


def _pack_col_scale():
    s = np.ones((1, N_PACK), np.float32)
    for mixer, sc in enumerate((HEAD_DIM ** -0.5, HEAD_DIM ** -0.5, DIFF_DIM ** -0.5, HEAD_DIM ** -0.5)):
        col = COL_QK + mixer * 2 * BRANCH_WIDTH
        s[:, col:col + BRANCH_WIDTH] = sc * LOG2E
    return jnp.asarray(s)


def _layer(x2, bsz, seq, layer, norm_w, w_pack, w_v, w_zg, w_misc, fox_b_f, lqk, subln_w, w_branch, w_out,
           tab_dsa, tab_diff, tab_dil):
    h = _rmsnorm(x2, norm_w, BF16)
    pack = _matmul(h, w_pack, BF16, PROJ_TM, PROJ_TN, col_scale=_pack_col_scale(),
                   name="proj_pack").reshape(bsz, seq, N_PACK)
    vt = _matmul(h, w_v, BF16, PROJ_TM, PROJ_TN, transpose_out=True, name="proj_vt")
    zg = _matmul(h, w_zg, BF16, PROJ_TM, PROJ_TN, name="proj_zg")
    misc = _matmul(h, w_misc, F32, PROJ_TM, N_MISC, name="proj_misc").reshape(bsz, seq, N_MISC)

    k2 = misc[:, :, :MISC_W].astype(BF16)
    w_t = jnp.transpose(misc[:, :, MISC_W:MISC_F], (0, 2, 1))
    o_dsa = _dsa_attention(pack, vt, w_t, k2, tab_dsa)

    f_bsh = misc[:, :, MISC_F:MISC_F + N_HEADS]
    cum_bhs = _fox_cum(jnp.transpose(f_bsh, (0, 2, 1)), fox_b_f)
    o_fox = _fox_attention(pack, vt, jnp.transpose(cum_bhs, (0, 2, 1)), cum_bhs)

    lambda_init = 0.8 - 0.6 * math.exp(-0.3 * layer)
    o_diff = _diff_attention(pack, vt, tab_diff, lqk, subln_w, lambda_init)
    o_dil = _dil_attention(pack, vt, tab_dil)

    m = bsz * seq
    branches = [o.reshape(m, BRANCH_WIDTH) for o in (o_dsa, o_fox, o_diff, o_dil)]
    merged = _merge(branches, zg, w_branch)
    return _matmul(merged, w_out, F32, PROJ_TM, PROJ_TN, residual=x2, name="out_proj")


def kernel(x, norm_w, w_in, fox_b_f, diff_lq1, diff_lk1, diff_lq2, diff_lk2, diff_subln_w, w_branch, w_out,
           rel_bias, final_norm_w):
    bsz, seq, d = x.shape
    w_pack, w_v, w_zg, w_misc = _pack_w_in(w_in)
    w_branch_bf = (w_branch * 0.5).astype(BF16)
    w_out_bf = w_out.astype(BF16)
    lqk = jnp.stack([diff_lq1, diff_lk1, diff_lq2, diff_lk2], axis=1)
    bias_dsa, bias_diff, bias_dil = jnp.split(rel_bias, 3, axis=-1)
    tab_dsa = _bias_tiles(bias_dsa, DSA_FIRST, DSA_TILES, causal=False)
    tab_diff = _bias_tiles(bias_diff, DIFF_FIRST, DIFF_TILES, causal=True)
    tab_dil = _dilated_tiles(bias_dil, DIL_FIRST, DIL_TILES)

    x2 = x.reshape(bsz * seq, d)
    for layer in range(DEPTH):
        x2 = _layer(x2, bsz, seq, layer, norm_w[layer], w_pack[layer], w_v[layer], w_zg[layer], w_misc[layer],
                    fox_b_f[layer], lqk[layer], diff_subln_w[layer], w_branch_bf[layer], w_out_bf[layer],
                    tab_dsa, tab_diff, tab_dil)
    return _rmsnorm(x2, final_norm_w, F32).reshape(bsz, seq, d)
```

```python
import functools
import math

import jax
import jax.numpy as jnp
import numpy as np
from jax import lax
from jax.experimental import pallas as pl
from jax.experimental.pallas import tpu as pltpu

F32 = jnp.float32
BF16 = jnp.bfloat16

D_MODEL = 2048
DEPTH = 4
HEAD_DIM = 128
N_HEADS = 4
BRANCH_WIDTH = N_HEADS * HEAD_DIM
N_BRANCHES = 4
DIFF_DIM = HEAD_DIM // 2
IDX_HEADS = 16
IDX_DIM = 64
TOPK_MAX = 256
DILATED_CONFIGS = ((128, 1), (512, 4), (2048, 16))
N_BUCKETS = 32
MAX_DISTANCE = 2048
RMS_EPS = 1e-6

LANES = 128
SUBLANES = 8
ONES_ROWS = 16
TQ = 256
TK = 256
TKF = 512
LOG2E = math.log2(math.e)
PROJ_TM, PROJ_TN = 2048, 512
NEG = -1e30
VMEM_LIMIT = 56 * 1024 * 1024

COL_IDXQ = 0
COL_QK = IDX_HEADS * IDX_DIM
N_PACK = COL_QK + N_BRANCHES * 2 * BRANCH_WIDTH
N_V = N_BRANCHES * BRANCH_WIDTH
N_ZG = N_BRANCHES * BRANCH_WIDTH + N_BRANCHES * D_MODEL
N_MISC = 256
MISC_W = 2 * IDX_DIM
MISC_F = MISC_W + IDX_HEADS
MIX_A, MIX_B, MIX_C, MIX_D = range(N_BRANCHES)

_NT = (((1,), (1,)), ((), ()))
_ONCE_PER_BATCH = pl.Buffered(1)


def _cparams(sem):
    return pltpu.CompilerParams(dimension_semantics=sem, vmem_limit_bytes=VMEM_LIMIT)


def _q_block(mixer):
    return COL_QK // BRANCH_WIDTH + 2 * mixer


def _rmsnorm_kernel(x_ref, w_ref, o_ref):
    x = x_ref[...]
    y = x * lax.rsqrt(jnp.mean(x * x, axis=-1, keepdims=True) + RMS_EPS)
    o_ref[...] = (y * w_ref[...]).astype(o_ref.dtype)


def _rmsnorm(x2, w, out_dtype, tm=512):
    m, d = x2.shape
    return pl.pallas_call(
        _rmsnorm_kernel,
        grid=(m // tm,),
        in_specs=[pl.BlockSpec((tm, d), lambda i: (i, 0)), pl.BlockSpec((1, d), lambda i: (0, 0))],
        out_specs=pl.BlockSpec((tm, d), lambda i: (i, 0)),
        out_shape=jax.ShapeDtypeStruct((m, d), out_dtype),
        compiler_params=_cparams(("parallel",)),
        name="rmsnorm",
    )(x2, w.reshape(1, d))


def _matmul_kernel(a_ref, b_ref, o_ref):
    o_ref[...] = jnp.dot(a_ref[...], b_ref[...], preferred_element_type=F32).astype(o_ref.dtype)


def _matmul_t_kernel(a_ref, b_ref, o_ref):
    o_ref[...] = jnp.dot(a_ref[...], b_ref[...], preferred_element_type=F32).T.astype(o_ref.dtype)


def _matmul_res_kernel(a_ref, b_ref, r_ref, o_ref):
    o_ref[...] = r_ref[...] + jnp.dot(a_ref[...], b_ref[...], preferred_element_type=F32)


def _matmul_scale_kernel(a_ref, b_ref, s_ref, o_ref):
    o_ref[...] = (jnp.dot(a_ref[...], b_ref[...], preferred_element_type=F32) * s_ref[...]).astype(o_ref.dtype)


def _matmul(a, b, out_dtype, tm, tn, residual=None, col_scale=None, transpose_out=False, name="matmul"):
    m, k = a.shape
    n = b.shape[1]
    tm, tn = min(tm, m), min(tn, n)
    in_specs = [pl.BlockSpec((tm, k), lambda i, j: (i, 0)), pl.BlockSpec((k, tn), lambda i, j: (0, j))]
    args = [a, b]
    kern = _matmul_kernel
    out_spec = pl.BlockSpec((tm, tn), lambda i, j: (i, j))
    out_shape = (m, n)
    if residual is not None:
        in_specs.append(pl.BlockSpec((tm, tn), lambda i, j: (i, j)))
        args.append(residual)
        kern = _matmul_res_kernel
    if col_scale is not None:
        in_specs.append(pl.BlockSpec((1, tn), lambda i, j: (0, j)))
        args.append(col_scale)
        kern = _matmul_scale_kernel
    if transpose_out:
        kern = _matmul_t_kernel
        out_spec = pl.BlockSpec((tn, tm), lambda i, j: (j, i))
        out_shape = (n, m)
    return pl.pallas_call(
        kern,
        grid=(m // tm, n // tn),
        in_specs=in_specs,
        out_specs=out_spec,
        out_shape=jax.ShapeDtypeStruct(out_shape, out_dtype),
        compiler_params=_cparams(("parallel", "arbitrary")),
        name=name,
    )(*args)


def _t5_bucket(dist):
    dist = jnp.maximum(dist, 0)
    max_exact = N_BUCKETS // 2
    d = jnp.maximum(dist, max_exact).astype(F32)
    large = max_exact + (jnp.log(d / max_exact) / math.log(MAX_DISTANCE / max_exact)
                         * (N_BUCKETS - max_exact)).astype(jnp.int32)
    large = jnp.minimum(large, N_BUCKETS - 1)
    return jnp.where(dist < max_exact, dist, large)


def _toeplitz_tiles(vals, n_tiles):
    nh, length = vals.shape
    hank = jnp.tile(vals, (1, LANES + 1))[:, :LANES * (length + 1)].reshape(nh, LANES, length + 1)
    hank = hank[:, ::-1, :n_tiles * LANES]
    return jnp.transpose(hank.reshape(nh, LANES, n_tiles, LANES), (0, 2, 1, 3))


def _distances(first, n_tiles):
    return first * LANES - (LANES - 1) + jnp.arange(n_tiles * LANES + LANES - 1)


def _bias_tiles(table, first, n_tiles, causal):
    d = _distances(first, n_tiles)
    vals = table[_t5_bucket(d)].T
    if causal:
        vals = jnp.where(d[None] >= 0, vals, NEG)
    return _toeplitz_tiles(vals * LOG2E, n_tiles)


def _dilated_tiles(table, first, n_tiles):
    d = _distances(first, n_tiles)
    mult = jnp.zeros(d.shape, F32)
    for window, dilation in DILATED_CONFIGS:
        mult = mult + ((d >= 0) & (d <= window) & (d % dilation == 0)).astype(F32)
    vals = table[_t5_bucket(d)].T + jnp.log(jnp.maximum(mult, 1.0))[None]
    return _toeplitz_tiles(jnp.where(mult[None] > 0, vals, NEG) * LOG2E, n_tiles)


def _bias_block(tab_ref, h, a, first, n_tiles, tk=TK):
    def tile(off):
        return tab_ref[h, jnp.clip(off - first, 0, n_tiles - 1)]
    rows = [jnp.concatenate([tile(a + ib - jb) for ib in range(TQ // LANES)], axis=1)
            for jb in range(tk // LANES)]
    return jnp.concatenate(rows, axis=0)


def _softmax_step(s, vt, m_ref, acc_ref, idx):
    m_prev = m_ref[idx]
    m_next = jnp.maximum(m_prev, jnp.max(s, axis=0, keepdims=True))
    alpha = jnp.exp2(m_prev - m_next)
    p = jnp.exp2(s - m_next).astype(BF16)
    vt_ones = jnp.concatenate([vt, jnp.ones((ONES_ROWS, vt.shape[1]), vt.dtype)], axis=0)
    m_ref[idx] = m_next
    acc_ref[idx] = acc_ref[idx] * alpha + jnp.dot(vt_ones, p, preferred_element_type=F32)


def _normalized(acc_ref, idx):
    acc = acc_ref[idx]
    return acc[:HEAD_DIM] / acc[HEAD_DIM:HEAD_DIM + 1]


def _pipelined(n, logits_fn, update_fn, lookahead=4):
    pending = {i: logits_fn(i) for i in range(min(lookahead, n))}
    for i in range(n):
        if i + lookahead < n:
            pending[i + lookahead] = logits_fn(i + lookahead)
        update_fn(i, pending.pop(i))


def _for_chunk_pairs(first, count, run):
    def body(p, carry):
        run([first + 2 * p, first + 2 * p + 1])
        return carry

    lax.fori_loop(0, count // 2, body, 0)

    @pl.when(count % 2 == 1)
    def _():
        run([first + count - 1])


def _init_softmax(m_ref, acc_ref):
    m_ref[...] = jnp.full(m_ref.shape, NEG, F32)
    acc_ref[...] = jnp.zeros(acc_ref.shape, F32)


def _softmax_scratch(n):
    return [pltpu.VMEM((n, 1, TQ), F32), pltpu.VMEM((n, HEAD_DIM + ONES_ROWS, TQ), F32)]


def _head(ref, h):
    return ref[0, :, h * HEAD_DIM:(h + 1) * HEAD_DIM]


def _store_head(o_ref, h, o_t):
    o_ref[0, :, h * HEAD_DIM:(h + 1) * HEAD_DIM] = o_t.T


def _logsig_cumsum_kernel(f_ref, b_ref, o_ref):
    x = f_ref[0, 0] + b_ref[0, 0]
    ls = jnp.minimum(x, 0.0) - jnp.log1p(jnp.exp(-jnp.abs(x)))
    rows = ls.shape[0]
    r = lax.broadcasted_iota(jnp.int32, (LANES, LANES), 0)
    c = lax.broadcasted_iota(jnp.int32, (LANES, LANES), 1)
    upper = (r <= c).astype(F32)
    within = jnp.dot(ls, upper, preferred_element_type=F32, precision=lax.Precision.HIGHEST)
    total = jnp.broadcast_to(within[:, LANES - 1:LANES], (rows, LANES))
    rr = lax.broadcasted_iota(jnp.int32, (rows, rows), 0)
    rc = lax.broadcasted_iota(jnp.int32, (rows, rows), 1)
    strict = (rc < rr).astype(F32)
    before = jnp.dot(strict, total, preferred_element_type=F32, precision=lax.Precision.HIGHEST)
    o_ref[0, 0] = within + before


def _fox_cum(f_bhs, b_f):
    bsz, nh, seq = f_bhs.shape
    rows = seq // LANES
    out = pl.pallas_call(
        _logsig_cumsum_kernel,
        grid=(bsz, nh),
        in_specs=[pl.BlockSpec((1, 1, rows, LANES), lambda b, h: (b, h, 0, 0)),
                  pl.BlockSpec((1, 1, 1, LANES), lambda b, h: (0, h, 0, 0))],
        out_specs=pl.BlockSpec((1, 1, rows, LANES), lambda b, h: (b, h, 0, 0)),
        out_shape=jax.ShapeDtypeStruct((bsz, nh, rows, LANES), F32),
        compiler_params=_cparams(("parallel", "parallel")),
        name="fox_cumsum",
    )(f_bhs.reshape(bsz, nh, rows, LANES), jnp.broadcast_to(b_f[None, :, None, None], (1, nh, 1, LANES)))
    return out.reshape(bsz, nh, seq)


def _last_key_tile(qi, tk):
    return (qi * TQ + TQ - 1) // tk


def _key_rows(c, tk):
    return pl.ds(pl.multiple_of(c * tk, tk), tk)


def _head_cols(h):
    return slice(h * HEAD_DIM, (h + 1) * HEAD_DIM)


def _seq_specs(mixer, seq):
    qb = _q_block(mixer)
    return [pl.BlockSpec((1, TQ, BRANCH_WIDTH), lambda b, i: (b, i, qb)),
            pl.BlockSpec((1, seq, BRANCH_WIDTH), lambda b, i: (b, 0, qb + 1), pipeline_mode=_ONCE_PER_BATCH),
            pl.BlockSpec((BRANCH_WIDTH, seq), lambda b, i: (mixer, b), pipeline_mode=_ONCE_PER_BATCH)]


def _fox_kernel(q_ref, k_ref, vt_ref, cq_ref, ck_ref, o_ref, m_ref, acc_ref):
    qi = pl.program_id(1)
    last = _last_key_tile(qi, TKF)
    _init_softmax(m_ref, acc_ref)

    def run(chunks, diagonal=False):
        rows = [_key_rows(c, TKF) for c in chunks]
        if diagonal:
            key = chunks[0] * TKF + lax.broadcasted_iota(jnp.int32, (TKF, TQ), 0)
            causal = key <= qi * TQ + lax.broadcasted_iota(jnp.int32, (TKF, TQ), 1)

        def logits(i):
            t, h = divmod(i, N_HEADS)
            decay = cq_ref[0, h:h + 1, :] * LOG2E - ck_ref[0, rows[t], h:h + 1] * LOG2E
            return decay + lax.dot_general(k_ref[0, rows[t], _head_cols(h)], _head(q_ref, h), _NT,
                                           preferred_element_type=F32)

        def update(i, s):
            t, h = divmod(i, N_HEADS)
            if diagonal:
                s = jnp.where(causal, s, NEG)
            _softmax_step(s, vt_ref[_head_cols(h), rows[t]], m_ref, acc_ref, h)

        _pipelined(len(chunks) * N_HEADS, logits, update)

    _for_chunk_pairs(0, last, run)
    run([last], diagonal=True)
    for h in range(N_HEADS):
        _store_head(o_ref, h, _normalized(acc_ref, h))


def _fox_attention(pack, vt, cum_bsh, cum_bhs):
    bsz, seq, _ = pack.shape
    return pl.pallas_call(
        _fox_kernel,
        grid=(bsz, seq // TQ),
        in_specs=_seq_specs(MIX_B, seq)
        + [pl.BlockSpec((1, N_HEADS, TQ), lambda b, i: (b, 0, i)),
           pl.BlockSpec((1, seq, N_HEADS), lambda b, i: (b, 0, 0))],
        out_specs=pl.BlockSpec((1, TQ, BRANCH_WIDTH), lambda b, i: (b, i, 0)),
        out_shape=jax.ShapeDtypeStruct((bsz, seq, BRANCH_WIDTH), F32),
        scratch_shapes=_softmax_scratch(N_HEADS),
        compiler_params=_cparams(("parallel", "arbitrary")),
        name="fox_attention",
    )(pack, pack, vt, cum_bhs, cum_bsh)


DIFF_FIRST, DIFF_TILES = -1, 15


def _diff_kernel(lambda_init, q_ref, k_ref, vt_ref, tab_ref, lqk_ref, sub_ref, o_ref, m_ref, acc_ref):
    qi = pl.program_id(1)
    _init_softmax(m_ref, acc_ref)
    lane = lax.broadcasted_iota(jnp.int32, (TQ, HEAD_DIM), 1)

    def run(chunks):
        rows = [_key_rows(c, TKF) for c in chunks]

        def logits(i):
            t, j = divmod(i, 2 * N_HEADS)
            h, m = divmod(j, 2)
            q = _head(q_ref, h)
            keep = lane < DIFF_DIM if m == 0 else lane >= DIFF_DIM
            qc = jnp.where(keep, q, jnp.zeros_like(q))
            bias = _bias_block(tab_ref, h, (qi * TQ - chunks[t] * TKF) // LANES, DIFF_FIRST, DIFF_TILES, TKF)
            return bias + lax.dot_general(k_ref[0, rows[t], _head_cols(h)], qc, _NT, preferred_element_type=F32)

        def update(i, s):
            t, j = divmod(i, 2 * N_HEADS)
            _softmax_step(s, vt_ref[_head_cols(j // 2), rows[t]], m_ref, acc_ref, j)

        _pipelined(len(chunks) * 2 * N_HEADS, logits, update)

    _for_chunk_pairs(0, _last_key_tile(qi, TKF) + 1, run)

    lqk = lqk_ref[...]
    lam = (jnp.exp(jnp.sum(lqk[0:1] * lqk[1:2], axis=1, keepdims=True))
           - jnp.exp(jnp.sum(lqk[2:3] * lqk[3:4], axis=1, keepdims=True)) + lambda_init)
    for h in range(N_HEADS):
        o = _normalized(acc_ref, 2 * h) - lam * _normalized(acc_ref, 2 * h + 1)
        y = o * lax.rsqrt(jnp.mean(o * o, axis=0, keepdims=True) + RMS_EPS) * sub_ref[...]
        _store_head(o_ref, h, y * (1.0 - lambda_init))


def _diff_attention(pack, vt, tab, lqk, subln_w, lambda_init):
    bsz, seq, _ = pack.shape
    sub = jnp.broadcast_to(subln_w[:, None], (HEAD_DIM, TQ))
    return pl.pallas_call(
        functools.partial(_diff_kernel, lambda_init),
        grid=(bsz, seq // TQ),
        in_specs=_seq_specs(MIX_C, seq)
        + [pl.BlockSpec(tab.shape, lambda b, i: (0, 0, 0, 0)),
           pl.BlockSpec(lqk.shape, lambda b, i: (0, 0)),
           pl.BlockSpec((HEAD_DIM, TQ), lambda b, i: (0, 0))],
        out_specs=pl.BlockSpec((1, TQ, BRANCH_WIDTH), lambda b, i: (b, i, 0)),
        out_shape=jax.ShapeDtypeStruct((bsz, seq, BRANCH_WIDTH), F32),
        scratch_shapes=_softmax_scratch(2 * N_HEADS),
        compiler_params=_cparams(("parallel", "arbitrary")),
        name="diff_attention",
    )(pack, pack, vt, tab, lqk, sub)


DIL_FIRST = -1
DIL_TILES = MAX_DISTANCE // LANES + 1 - DIL_FIRST + 1


def _dil_kernel(q_ref, k_ref, vt_ref, tab_ref, o_ref, m_ref, acc_ref):
    qi = pl.program_id(1)
    _init_softmax(m_ref, acc_ref)

    first = jnp.maximum(qi * TQ - MAX_DISTANCE, 0) // TKF
    last = _last_key_tile(qi, TKF)

    def run(chunks):
        rows = [_key_rows(c, TKF) for c in chunks]

        def logits(i):
            t, h = divmod(i, N_HEADS)
            bias = _bias_block(tab_ref, h, (qi * TQ - chunks[t] * TKF) // LANES, DIL_FIRST, DIL_TILES, TKF)
            return bias + lax.dot_general(k_ref[0, rows[t], _head_cols(h)], _head(q_ref, h), _NT,
                                          preferred_element_type=F32)

        def update(i, s):
            t, h = divmod(i, N_HEADS)
            _softmax_step(s, vt_ref[_head_cols(h), rows[t]], m_ref, acc_ref, h)

        _pipelined(len(chunks) * N_HEADS, logits, update)

    _for_chunk_pairs(first, last - first + 1, run)
    for h in range(N_HEADS):
        _store_head(o_ref, h, _normalized(acc_ref, h))


def _dil_attention(pack, vt, tab):
    bsz, seq, _ = pack.shape
    return pl.pallas_call(
        _dil_kernel,
        grid=(bsz, seq // TQ),
        in_specs=_seq_specs(MIX_D, seq) + [pl.BlockSpec(tab.shape, lambda b, i: (0, 0, 0, 0))],
        out_specs=pl.BlockSpec((1, TQ, BRANCH_WIDTH), lambda b, i: (b, i, 0)),
        out_shape=jax.ShapeDtypeStruct((bsz, seq, BRANCH_WIDTH), F32),
        scratch_shapes=_softmax_scratch(N_HEADS),
        compiler_params=_cparams(("parallel", "arbitrary")),
        name="dilated_attention",
    )(pack, pack, vt, tab)


DSA_FIRST, DSA_TILES = -1, 15
LOWEST = -3.0e38
CHUNK = 2 * TK
ACC_ROWS = 4 * SUBLANES
COARSE_ITERS = 10


def _fold(x, op):
    return op(x.reshape(CHUNK // ACC_ROWS, ACC_ROWS, TQ), axis=0)


def _dsa_kernel(topk, iq_ref, wt_ref, k2_ref, q_ref, k_ref, vt_ref, tab_ref, o_ref,
                sc_ref, scb_ref, thr_ref, m_ref, acc_ref):
    qi = pl.program_id(1)
    n_tiles = qi + 1
    lane = lax.broadcasted_iota(jnp.int32, (TQ, 2 * IDX_DIM), 1)
    wt = wt_ref[0] * (IDX_HEADS ** -0.5 * IDX_DIM ** -0.5)
    n_chunks = (n_tiles + 1) // 2
    key_c = lax.broadcasted_iota(jnp.int32, (CHUNK, TQ), 0)
    qry_c = lax.broadcasted_iota(jnp.int32, (CHUNK, TQ), 1)

    def chunk_rows(c):
        return pl.ds(pl.multiple_of(c * CHUNK, CHUNK), CHUNK)

    def score_chunk(c, last, hi_lo):
        k2 = k2_ref[0, chunk_rows(c), :]
        acc = jnp.zeros((CHUNK, TQ), F32)
        for pair in range(IDX_HEADS // 2):
            qp = iq_ref[0, :, pair * 2 * IDX_DIM:(pair + 1) * 2 * IDX_DIM]
            for half in range(2):
                keep = lane < IDX_DIM if half == 0 else lane >= IDX_DIM
                qc = jnp.where(keep, qp, jnp.zeros_like(qp))
                rel = jnp.maximum(lax.dot_general(k2, qc, _NT, preferred_element_type=F32), 0.0)
                hh = 2 * pair + half
                acc = acc + wt[hh:hh + 1, :] * rel
        low = acc
        if last:
            causal = c * CHUNK + key_c <= qi * TQ + qry_c
            low = jnp.where(causal, acc, jnp.inf)
            acc = jnp.where(causal, acc, -jnp.inf)
        sc_ref[chunk_rows(c), :] = acc
        scb_ref[chunk_rows(c), :] = acc.astype(BF16)
        return jnp.maximum(hi_lo[0], _fold(acc, jnp.max)), jnp.minimum(hi_lo[1], _fold(low, jnp.min))

    hi_lo = (jnp.full((ACC_ROWS, TQ), -jnp.inf, F32), jnp.full((ACC_ROWS, TQ), jnp.inf, F32))
    hi_lo = lax.fori_loop(0, n_chunks - 1, lambda c, st: score_chunk(c, False, st), hi_lo)
    hi_lo = score_chunk(n_chunks - 1, True, hi_lo)
    row_max = jnp.max(hi_lo[0], axis=0, keepdims=True)
    row_min = jnp.min(hi_lo[1], axis=0, keepdims=True)

    def reduce_tiles(fn, init, reduce, combine):
        def body(c, carry):
            return combine(carry, _fold(fn(sc_ref[chunk_rows(c), :], c), reduce))
        return reduce(lax.fori_loop(0, n_chunks, body, init), axis=0, keepdims=True)

    def count(pred):
        return reduce_tiles(lambda blk, kt: jnp.where(pred(blk, kt), 1.0, 0.0),
                            jnp.zeros((ACC_ROWS, TQ), F32), jnp.sum, jnp.add)

    t_q = qi * TQ + lax.broadcasted_iota(jnp.int32, (1, TQ), 1)
    few = t_q < topk
    kf = float(topk)

    def count_b(mid_b):
        def body(c, carry):
            blk = scb_ref[chunk_rows(c), :]
            ind = jnp.where(blk >= mid_b, jnp.ones_like(blk), jnp.zeros_like(blk))
            parts = ind.reshape(CHUNK // ACC_ROWS, ACC_ROWS, TQ)
            for g in range(CHUNK // ACC_ROWS):
                carry = carry + parts[g]
            return carry
        acc = lax.fori_loop(0, n_chunks, body, jnp.zeros((ACC_ROWS, TQ), BF16))
        return jnp.sum(acc.astype(F32), axis=0, keepdims=True)

    def coarse(_, st):
        lo, hi = st
        mid_b = (lo + (hi - lo) * 0.5).astype(BF16)
        ge = count_b(mid_b) >= kf
        mid = mid_b.astype(F32)
        return jnp.where(ge, mid, lo), jnp.where(ge, hi, mid)

    lo_b = row_min.astype(BF16).astype(F32)
    hi_b = (row_max + (jnp.abs(row_max) * 2.0 ** -6 + 1e-30)).astype(BF16).astype(F32)
    lo_b, hi_b = lax.fori_loop(0, COARSE_ITERS, coarse, (lo_b, hi_b))
    lo0 = lo_b - (jnp.abs(lo_b) * 2.0 ** -8 + 1e-30)
    hi0 = hi_b + (jnp.abs(hi_b) * 2.0 ** -8 + 1e-30)

    def cond(st):
        return jnp.logical_and(st[0] < 400, jnp.min(st[4]) < 0.5)

    def body(st):
        it, lo, hi, thr, done, tie = st
        mid = lo + (hi - lo) * 0.5
        cnt = count(lambda blk, kt: blk >= mid)
        live = done < 0.5
        exact = jnp.logical_and(live, cnt == kf)
        stuck = jnp.logical_and(jnp.logical_and(live, cnt != kf), jnp.logical_or(mid <= lo, mid >= hi))
        thr = jnp.where(exact, mid, jnp.where(stuck, lo, thr))
        tie = jnp.where(stuck, 1.0, tie)
        done = jnp.where(jnp.logical_or(exact, stuck), 1.0, done)
        lo = jnp.where(cnt > kf, mid, lo)
        hi = jnp.where(cnt < kf, mid, hi)
        return it + 1, lo, hi, thr, done, tie

    init = (jnp.int32(0), lo0, hi0,
            jnp.full((1, TQ), LOWEST, F32), jnp.where(few, 1.0, 0.0), jnp.zeros((1, TQ), F32))
    _, _, _, thr, _, tie = lax.while_loop(cond, body, init)
    thr_ref[...] = jnp.broadcast_to(thr, thr_ref.shape)

    @pl.when(jnp.max(tie) > 0.5)
    def _():
        is_tie = tie > 0.5
        need = kf - count(lambda blk, kt: blk > thr)

        def kept(limit):
            return count(lambda blk, c: jnp.logical_and(blk == thr, c * CHUNK + key_c <= limit))

        def jbody(_, st):
            jlo, jhi = st
            jmid = (jlo + jhi) // 2
            ok = kept(jmid) >= need
            return jnp.where(ok, jlo, jmid), jnp.where(ok, jmid, jhi)

        jlo0 = jnp.full((1, TQ), -1, jnp.int32)
        jhi0 = jnp.zeros((1, TQ), jnp.int32) + (n_chunks * CHUNK - 1)
        _, cut = lax.fori_loop(0, 14, jbody, (jlo0, jhi0))

        def drop(c, carry):
            blk = sc_ref[chunk_rows(c), :]
            extra = jnp.logical_and(jnp.logical_and(is_tie, blk == thr), c * CHUNK + key_c > cut)
            sc_ref[chunk_rows(c), :] = jnp.where(extra, -jnp.inf, blk)
            return carry

        lax.fori_loop(0, n_chunks, drop, 0)

    _init_softmax(m_ref, acc_ref)

    def attend(chunks):
        rows = [chunk_rows(c) for c in chunks]
        sel = [sc_ref[r, :] >= thr_ref[0:1, :] for r in rows]

        def logits(i):
            t, h = divmod(i, N_HEADS)
            bias = _bias_block(tab_ref, h, (qi * TQ - chunks[t] * CHUNK) // LANES, DSA_FIRST, DSA_TILES, CHUNK)
            return bias + lax.dot_general(k_ref[0, rows[t], _head_cols(h)], _head(q_ref, h), _NT,
                                          preferred_element_type=F32)

        def update(i, s):
            t, h = divmod(i, N_HEADS)
            _softmax_step(jnp.where(sel[t], s, NEG), vt_ref[_head_cols(h), rows[t]], m_ref, acc_ref, h)

        _pipelined(len(chunks) * N_HEADS, logits, update)

    _for_chunk_pairs(0, n_chunks, attend)
    for h in range(N_HEADS):
        _store_head(o_ref, h, _normalized(acc_ref, h))


def _dsa_attention(pack, vt, w_t, k2, tab):
    bsz, seq, _ = pack.shape
    nq = seq // TQ
    topk = min(TOPK_MAX, seq // 4)
    qb = _q_block(MIX_A)
    return pl.pallas_call(
        functools.partial(_dsa_kernel, topk),
        grid=(bsz, nq),
        in_specs=[pl.BlockSpec((1, TQ, IDX_HEADS * IDX_DIM), lambda b, i: (b, i, 0)),
                  pl.BlockSpec((1, IDX_HEADS, TQ), lambda b, i: (b, 0, i)),
                  pl.BlockSpec((1, seq, 2 * IDX_DIM), lambda b, i: (b, 0, 0), pipeline_mode=_ONCE_PER_BATCH),
                  pl.BlockSpec((1, TQ, BRANCH_WIDTH), lambda b, i: (b, i, qb)),
                  pl.BlockSpec((1, seq, BRANCH_WIDTH), lambda b, i: (b, 0, qb + 1), pipeline_mode=_ONCE_PER_BATCH),
                  pl.BlockSpec((BRANCH_WIDTH, seq), lambda b, i: (MIX_A, b), pipeline_mode=_ONCE_PER_BATCH),
                  pl.BlockSpec(tab.shape, lambda b, i: (0, 0, 0, 0))],
        out_specs=pl.BlockSpec((1, TQ, BRANCH_WIDTH), lambda b, i: (b, i, 0)),
        out_shape=jax.ShapeDtypeStruct((bsz, seq, BRANCH_WIDTH), F32),
        scratch_shapes=[pltpu.VMEM((seq, TQ), F32), pltpu.VMEM((seq, TQ), BF16), pltpu.VMEM((SUBLANES, TQ), F32)]
        + _softmax_scratch(N_HEADS),
        compiler_params=_cparams(("parallel", "arbitrary")),
        name="dsa_attention",
    )(pack, w_t, k2, pack, pack, vt, tab)


def _merge_kernel(oa_ref, ob_ref, oc_ref, od_ref, z_ref, g0_ref, g1_ref, g2_ref, g3_ref, wb_ref, o_ref):
    merged = None
    for b, (o_ref_b, g_ref) in enumerate(zip((oa_ref, ob_ref, oc_ref, od_ref),
                                             (g0_ref, g1_ref, g2_ref, g3_ref))):
        zh = z_ref[:, b * BRANCH_WIDTH:(b + 1) * BRANCH_WIDTH].astype(F32)
        y = o_ref_b[...] * (zh * jnp.tanh(zh) + zh)
        half = jnp.dot(y.astype(BF16), wb_ref[b], preferred_element_type=F32)
        term = jnp.tanh(g_ref[...].astype(F32)) * half + half
        merged = term if merged is None else merged + term
    o_ref[...] = merged.astype(o_ref.dtype)


def _merge(branches, zg, w_branch, tm=256):
    m = zg.shape[0]
    o_spec = pl.BlockSpec((tm, BRANCH_WIDTH), lambda i: (i, 0))
    g_specs = [pl.BlockSpec((tm, D_MODEL), functools.partial(lambda i, b: (i, 1 + b), b=b))
               for b in range(N_BRANCHES)]
    return pl.pallas_call(
        _merge_kernel,
        grid=(m // tm,),
        in_specs=[o_spec] * N_BRANCHES + [pl.BlockSpec((tm, N_BRANCHES * BRANCH_WIDTH), lambda i: (i, 0))]
        + g_specs + [pl.BlockSpec(w_branch.shape, lambda i: (0, 0, 0))],
        out_specs=pl.BlockSpec((tm, D_MODEL), lambda i: (i, 0)),
        out_shape=jax.ShapeDtypeStruct((m, D_MODEL), BF16),
        compiler_params=_cparams(("parallel",)),
        name="gated_merge",
    )(*branches, zg, zg, zg, zg, zg, w_branch)


def _pack_w_in(w_in):
    sizes = (N_BRANCHES * BRANCH_WIDTH, N_BRANCHES * D_MODEL, 3 * BRANCH_WIDTH, IDX_HEADS * IDX_DIM,
             IDX_DIM, IDX_HEADS, 3 * BRANCH_WIDTH, N_HEADS, 3 * BRANCH_WIDTH, 3 * BRANCH_WIDTH)
    offs = np.concatenate([[0], np.cumsum(sizes)])
    w = w_in.astype(BF16)
    z, gate, a, iq, ik, iw, b, ff, c, d = [w[:, :, offs[n]:offs[n + 1]] for n in range(len(sizes))]
    qk = [m[:, :, :2 * BRANCH_WIDTH] for m in (a, b, c, d)]
    v = [m[:, :, 2 * BRANCH_WIDTH:] for m in (a, b, c, d)]
    pad = jnp.zeros(w.shape[:2] + (N_MISC - MISC_F - N_HEADS,), w.dtype)
    half = jnp.asarray(0.5, BF16)
    return (jnp.concatenate([iq] + qk, axis=-1), jnp.concatenate(v, axis=-1),
            jnp.concatenate([z * half, gate * half], axis=-1), jnp.concatenate([ik, ik, iw, ff, pad], axis=-1))


def _pack_col_scale():
    s = np.ones((1, N_PACK), np.float32)
    for mixer, sc in enumerate((HEAD_DIM ** -0.5, HEAD_DIM ** -0.5, DIFF_DIM ** -0.5, HEAD_DIM ** -0.5)):
        col = COL_QK + mixer * 2 * BRANCH_WIDTH
        s[:, col:col + BRANCH_WIDTH] = sc * LOG2E
    return jnp.asarray(s)


def _layer(x2, bsz, seq, layer, norm_w, w_pack, w_v, w_zg, w_misc, fox_b_f, lqk, subln_w, w_branch, w_out,
           tab_dsa, tab_diff, tab_dil):
    h = _rmsnorm(x2, norm_w, BF16)
    pack = _matmul(h, w_pack, BF16, PROJ_TM, PROJ_TN, col_scale=_pack_col_scale(),
                   name="proj_pack").reshape(bsz, seq, N_PACK)
    vt = _matmul(h, w_v, BF16, PROJ_TM, PROJ_TN, transpose_out=True, name="proj_vt")
    zg = _matmul(h, w_zg, BF16, PROJ_TM, PROJ_TN, name="proj_zg")
    misc = _matmul(h, w_misc, F32, PROJ_TM, N_MISC, name="proj_misc").reshape(bsz, seq, N_MISC)

    k2 = misc[:, :, :MISC_W].astype(BF16)
    w_t = jnp.transpose(misc[:, :, MISC_W:MISC_F], (0, 2, 1))
    o_dsa = _dsa_attention(pack, vt, w_t, k2, tab_dsa)

    f_bsh = misc[:, :, MISC_F:MISC_F + N_HEADS]
    cum_bhs = _fox_cum(jnp.transpose(f_bsh, (0, 2, 1)), fox_b_f)
    o_fox = _fox_attention(pack, vt, jnp.transpose(cum_bhs, (0, 2, 1)), cum_bhs)

    lambda_init = 0.8 - 0.6 * math.exp(-0.3 * layer)
    o_diff = _diff_attention(pack, vt, tab_diff, lqk, subln_w, lambda_init)
    o_dil = _dil_attention(pack, vt, tab_dil)

    m = bsz * seq
    branches = [o.reshape(m, BRANCH_WIDTH) for o in (o_dsa, o_fox, o_diff, o_dil)]
    merged = _merge(branches, zg, w_branch)
    return _matmul(merged, w_out, F32, PROJ_TM, PROJ_TN, residual=x2, name="out_proj")


def kernel(x, norm_w, w_in, fox_b_f, diff_lq1, diff_lk1, diff_lq2, diff_lk2, diff_subln_w, w_branch, w_out,
           rel_bias, final_norm_w):
    bsz, seq, d = x.shape
    w_pack, w_v, w_zg, w_misc = _pack_w_in(w_in)
    w_branch_bf = (w_branch * 0.5).astype(BF16)
    w_out_bf = w_out.astype(BF16)
    lqk = jnp.stack([diff_lq1, diff_lk1, diff_lq2, diff_lk2], axis=1)
    bias_dsa, bias_diff, bias_dil = jnp.split(rel_bias, 3, axis=-1)
    tab_dsa = _bias_tiles(bias_dsa, DSA_FIRST, DSA_TILES, causal=False)
    tab_diff = _bias_tiles(bias_diff, DIFF_FIRST, DIFF_TILES, causal=True)
    tab_dil = _dilated_tiles(bias_dil, DIL_FIRST, DIL_TILES)

    x2 = x.reshape(bsz * seq, d)
    for layer in range(DEPTH):
        x2 = _layer(x2, bsz, seq, layer, norm_w[layer], w_pack[layer], w_v[layer], w_zg[layer], w_misc[layer],
                    fox_b_f[layer], lqk[layer], diff_subln_w[layer], w_branch_bf[layer], w_out_bf[layer],
                    tab_dsa, tab_diff, tab_dil)
    return _rmsnorm(x2, final_norm_w, F32).reshape(bsz, seq, d)
```

```python
import functools
import math

import jax
import jax.numpy as jnp
import numpy as np
from jax import lax
from jax.experimental import pallas as pl
from jax.experimental.pallas import tpu as pltpu

F32 = jnp.float32
BF16 = jnp.bfloat16

D_MODEL = 2048
DEPTH = 4
HEAD_DIM = 128
N_HEADS = 4
BRANCH_WIDTH = N_HEADS * HEAD_DIM
N_BRANCHES = 4
DIFF_DIM = HEAD_DIM // 2
IDX_HEADS = 16
IDX_DIM = 64
TOPK_MAX = 256
DILATED_CONFIGS = ((128, 1), (512, 4), (2048, 16))
N_BUCKETS = 32
MAX_DISTANCE = 2048
RMS_EPS = 1e-6

LANES = 128
SUBLANES = 8
ONES_ROWS = 16
TQ = 256
TKF = 512
LOG2E = math.log2(math.e)
PROJ_TM, PROJ_TN = 2048, 512
NEG = -1e30
VMEM_LIMIT = 56 * 1024 * 1024

COL_IDXQ = 0
COL_QK = IDX_HEADS * IDX_DIM
N_PACK = COL_QK + N_BRANCHES * 2 * BRANCH_WIDTH
N_V = N_BRANCHES * BRANCH_WIDTH
N_ZG = N_BRANCHES * BRANCH_WIDTH + N_BRANCHES * D_MODEL
N_MISC = 256
MISC_W = 2 * IDX_DIM
MISC_F = MISC_W + IDX_HEADS
MIX_A, MIX_B, MIX_C, MIX_D = range(N_BRANCHES)

_NT = (((1,), (1,)), ((), ()))
_ONCE_PER_BATCH = pl.Buffered(1)


def _cparams(sem):
    return pltpu.CompilerParams(dimension_semantics=sem, vmem_limit_bytes=VMEM_LIMIT)


def _q_block(mixer):
    return COL_QK // BRANCH_WIDTH + 2 * mixer


def _rmsnorm_kernel(x_ref, w_ref, o_ref):
    x = x_ref[...]
    y = x * lax.rsqrt(jnp.mean(x * x, axis=-1, keepdims=True) + RMS_EPS)
    o_ref[...] = (y * w_ref[...]).astype(o_ref.dtype)


def _rmsnorm(x2, w, out_dtype, tm=512):
    m, d = x2.shape
    return pl.pallas_call(
        _rmsnorm_kernel,
        grid=(m // tm,),
        in_specs=[pl.BlockSpec((tm, d), lambda i: (i, 0)), pl.BlockSpec((1, d), lambda i: (0, 0))],
        out_specs=pl.BlockSpec((tm, d), lambda i: (i, 0)),
        out_shape=jax.ShapeDtypeStruct((m, d), out_dtype),
        compiler_params=_cparams(("parallel",)),
        name="rmsnorm",
    )(x2, w.reshape(1, d))


def _matmul_kernel(a_ref, b_ref, o_ref):
    o_ref[...] = jnp.dot(a_ref[...], b_ref[...], preferred_element_type=F32).astype(o_ref.dtype)


def _matmul_t_kernel(a_ref, b_ref, o_ref):
    o_ref[...] = jnp.dot(a_ref[...], b_ref[...], preferred_element_type=F32).T.astype(o_ref.dtype)


def _matmul_res_kernel(a_ref, b_ref, r_ref, o_ref):
    o_ref[...] = r_ref[...] + jnp.dot(a_ref[...], b_ref[...], preferred_element_type=F32)


def _matmul_scale_kernel(a_ref, b_ref, s_ref, o_ref):
    o_ref[...] = (jnp.dot(a_ref[...], b_ref[...], preferred_element_type=F32) * s_ref[...]).astype(o_ref.dtype)


def _matmul(a, b, out_dtype, tm, tn, residual=None, col_scale=None, transpose_out=False, name="matmul"):
    m, k = a.shape
    n = b.shape[1]
    tm, tn = min(tm, m), min(tn, n)
    in_specs = [pl.BlockSpec((tm, k), lambda i, j: (i, 0)), pl.BlockSpec((k, tn), lambda i, j: (0, j))]
    args = [a, b]
    kern = _matmul_kernel
    out_spec = pl.BlockSpec((tm, tn), lambda i, j: (i, j))
    out_shape = (m, n)
    if residual is not None:
        in_specs.append(pl.BlockSpec((tm, tn), lambda i, j: (i, j)))
        args.append(residual)
        kern = _matmul_res_kernel
    if col_scale is not None:
        in_specs.append(pl.BlockSpec((1, tn), lambda i, j: (0, j)))
        args.append(col_scale)
        kern = _matmul_scale_kernel
    if transpose_out:
        kern = _matmul_t_kernel
        out_spec = pl.BlockSpec((tn, tm), lambda i, j: (j, i))
        out_shape = (n, m)
    return pl.pallas_call(
        kern,
        grid=(m // tm, n // tn),
        in_specs=in_specs,
        out_specs=out_spec,
        out_shape=jax.ShapeDtypeStruct(out_shape, out_dtype),
        compiler_params=_cparams(("parallel", "arbitrary")),
        name=name,
    )(*args)


def _t5_bucket(dist):
    dist = jnp.maximum(dist, 0)
    max_exact = N_BUCKETS // 2
    d = jnp.maximum(dist, max_exact).astype(F32)
    large = max_exact + (jnp.log(d / max_exact) / math.log(MAX_DISTANCE / max_exact)
                         * (N_BUCKETS - max_exact)).astype(jnp.int32)
    large = jnp.minimum(large, N_BUCKETS - 1)
    return jnp.where(dist < max_exact, dist, large)


def _toeplitz_tiles(vals, n_tiles):
    nh, length = vals.shape
    hank = jnp.tile(vals, (1, LANES + 1))[:, :LANES * (length + 1)].reshape(nh, LANES, length + 1)
    hank = hank[:, ::-1, :n_tiles * LANES]
    return jnp.transpose(hank.reshape(nh, LANES, n_tiles, LANES), (0, 2, 1, 3))


def _distances(first, n_tiles):
    return first * LANES - (LANES - 1) + jnp.arange(n_tiles * LANES + LANES - 1)


def _bias_tiles(table, first, n_tiles, causal):
    d = _distances(first, n_tiles)
    vals = table[_t5_bucket(d)].T
    if causal:
        vals = jnp.where(d[None] >= 0, vals, NEG)
    return _toeplitz_tiles(vals * LOG2E, n_tiles)


def _dilated_tiles(table, first, n_tiles):
    d = _distances(first, n_tiles)
    mult = jnp.zeros(d.shape, F32)
    for window, dilation in DILATED_CONFIGS:
        mult = mult + ((d >= 0) & (d <= window) & (d % dilation == 0)).astype(F32)
    vals = table[_t5_bucket(d)].T + jnp.log(jnp.maximum(mult, 1.0))[None]
    return _toeplitz_tiles(jnp.where(mult[None] > 0, vals, NEG) * LOG2E, n_tiles)


def _bias_block(tab_ref, h, a, first, n_tiles):
    def tile(off):
        return tab_ref[h, jnp.clip(off - first, 0, n_tiles - 1)]
    rows = [jnp.concatenate([tile(a + ib - jb) for ib in range(TQ // LANES)], axis=1)
            for jb in range(TKF // LANES)]
    return jnp.concatenate(rows, axis=0)


def _softmax_step(s, vt, m_ref, acc_ref, idx):
    m_prev = m_ref[idx]
    m_next = jnp.maximum(m_prev, jnp.max(s, axis=0, keepdims=True))
    alpha = jnp.exp2(m_prev - m_next)
    p = jnp.exp2(s - m_next).astype(BF16)
    vt_ones = jnp.concatenate([vt, jnp.ones((ONES_ROWS, vt.shape[1]), vt.dtype)], axis=0)
    m_ref[idx] = m_next
    acc_ref[idx] = acc_ref[idx] * alpha + jnp.dot(vt_ones, p, preferred_element_type=F32)


def _normalized(acc_ref, idx):
    acc = acc_ref[idx]
    return acc[:HEAD_DIM] / acc[HEAD_DIM:HEAD_DIM + 1]


def _pipelined(n, logits_fn, update_fn, lookahead=4):
    pending = {i: logits_fn(i) for i in range(min(lookahead, n))}
    for i in range(n):
        if i + lookahead < n:
            pending[i + lookahead] = logits_fn(i + lookahead)
        update_fn(i, pending.pop(i))


def _for_chunk_groups(first, count, run, group=2):
    def body(p, carry):
        run([first + group * p + g for g in range(group)])
        return carry

    lax.fori_loop(0, count // group, body, 0)

    def rest(c, carry):
        run([first + c])
        return carry

    lax.fori_loop(count - count % group, count, rest, 0)


def _init_softmax(m_ref, acc_ref):
    m_ref[...] = jnp.full(m_ref.shape, NEG, F32)
    acc_ref[...] = jnp.zeros(acc_ref.shape, F32)


def _softmax_scratch(n):
    return [pltpu.VMEM((n, 1, TQ), F32), pltpu.VMEM((n, HEAD_DIM + ONES_ROWS, TQ), F32)]


def _head(ref, h):
    return ref[0, :, h * HEAD_DIM:(h + 1) * HEAD_DIM]


def _store_head(o_ref, h, o_t):
    o_ref[0, :, h * HEAD_DIM:(h + 1) * HEAD_DIM] = o_t.T


def _logsig_cumsum_kernel(f_ref, b_ref, o_ref):
    x = f_ref[0, 0] + b_ref[0, 0]
    ls = jnp.minimum(x, 0.0) - jnp.log1p(jnp.exp(-jnp.abs(x)))
    rows = ls.shape[0]
    r = lax.broadcasted_iota(jnp.int32, (LANES, LANES), 0)
    c = lax.broadcasted_iota(jnp.int32, (LANES, LANES), 1)
    upper = (r <= c).astype(F32)
    within = jnp.dot(ls, upper, preferred_element_type=F32, precision=lax.Precision.HIGHEST)
    total = jnp.broadcast_to(within[:, LANES - 1:LANES], (rows, LANES))
    rr = lax.broadcasted_iota(jnp.int32, (rows, rows), 0)
    rc = lax.broadcasted_iota(jnp.int32, (rows, rows), 1)
    strict = (rc < rr).astype(F32)
    before = jnp.dot(strict, total, preferred_element_type=F32, precision=lax.Precision.HIGHEST)
    o_ref[0, 0] = within + before


def _fox_cum(f_bhs, b_f):
    bsz, nh, seq = f_bhs.shape
    rows = seq // LANES
    out = pl.pallas_call(
        _logsig_cumsum_kernel,
        grid=(bsz, nh),
        in_specs=[pl.BlockSpec((1, 1, rows, LANES), lambda b, h: (b, h, 0, 0)),
                  pl.BlockSpec((1, 1, 1, LANES), lambda b, h: (0, h, 0, 0))],
        out_specs=pl.BlockSpec((1, 1, rows, LANES), lambda b, h: (b, h, 0, 0)),
        out_shape=jax.ShapeDtypeStruct((bsz, nh, rows, LANES), F32),
        compiler_params=_cparams(("parallel", "parallel")),
        name="fox_cumsum",
    )(f_bhs.reshape(bsz, nh, rows, LANES), jnp.broadcast_to(b_f[None, :, None, None], (1, nh, 1, LANES)))
    return out.reshape(bsz, nh, seq)


def _last_key_tile(qi, tk):
    return (qi * TQ + TQ - 1) // tk


def _key_rows(c, tk):
    return pl.ds(pl.multiple_of(c * tk, tk), tk)


def _head_cols(h):
    return slice(h * HEAD_DIM, (h + 1) * HEAD_DIM)


def _seq_specs(mixer, seq):
    qb = _q_block(mixer)
    return [pl.BlockSpec((1, TQ, BRANCH_WIDTH), lambda b, i: (b, i, qb)),
            pl.BlockSpec((1, seq, BRANCH_WIDTH), lambda b, i: (b, 0, qb + 1), pipeline_mode=_ONCE_PER_BATCH),
            pl.BlockSpec((BRANCH_WIDTH, seq), lambda b, i: (mixer, b), pipeline_mode=_ONCE_PER_BATCH)]


def _fox_kernel(q_ref, k_ref, vt_ref, cq_ref, ck_ref, o_ref, m_ref, acc_ref):
    qi = pl.program_id(1)
    last = _last_key_tile(qi, TKF)
    _init_softmax(m_ref, acc_ref)

    def run(chunks, diagonal=False):
        rows = [_key_rows(c, TKF) for c in chunks]
        if diagonal:
            key = chunks[0] * TKF + lax.broadcasted_iota(jnp.int32, (TKF, TQ), 0)
            causal = key <= qi * TQ + lax.broadcasted_iota(jnp.int32, (TKF, TQ), 1)

        def logits(i):
            t, h = divmod(i, N_HEADS)
            decay = cq_ref[0, h:h + 1, :] * LOG2E - ck_ref[0, rows[t], h:h + 1] * LOG2E
            return decay + lax.dot_general(k_ref[0, rows[t], _head_cols(h)], _head(q_ref, h), _NT,
                                           preferred_element_type=F32)

        def update(i, s):
            t, h = divmod(i, N_HEADS)
            if diagonal:
                s = jnp.where(causal, s, NEG)
            _softmax_step(s, vt_ref[_head_cols(h), rows[t]], m_ref, acc_ref, h)

        _pipelined(len(chunks) * N_HEADS, logits, update)

    _for_chunk_groups(0, last, run, group=3)
    run([last], diagonal=True)
    for h in range(N_HEADS):
        _store_head(o_ref, h, _normalized(acc_ref, h))


def _fox_attention(pack, vt, cum_bsh, cum_bhs):
    bsz, seq, _ = pack.shape
    return pl.pallas_call(
        _fox_kernel,
        grid=(bsz, seq // TQ),
        in_specs=_seq_specs(MIX_B, seq)
        + [pl.BlockSpec((1, N_HEADS, TQ), lambda b, i: (b, 0, i)),
           pl.BlockSpec((1, seq, N_HEADS), lambda b, i: (b, 0, 0))],
        out_specs=pl.BlockSpec((1, TQ, BRANCH_WIDTH), lambda b, i: (b, i, 0)),
        out_shape=jax.ShapeDtypeStruct((bsz, seq, BRANCH_WIDTH), F32),
        scratch_shapes=_softmax_scratch(N_HEADS),
        compiler_params=_cparams(("parallel", "arbitrary")),
        name="fox_attention",
    )(pack, pack, vt, cum_bhs, cum_bsh)


DIFF_FIRST, DIFF_TILES = -1, 15


def _diff_kernel(lambda_init, q_ref, k_ref, vt_ref, tab_ref, lqk_ref, sub_ref, o_ref, m_ref, acc_ref):
    qi = pl.program_id(1)
    _init_softmax(m_ref, acc_ref)
    lane = lax.broadcasted_iota(jnp.int32, (TQ, HEAD_DIM), 1)

    def run(chunks):
        rows = [_key_rows(c, TKF) for c in chunks]

        def logits(i):
            t, j = divmod(i, 2 * N_HEADS)
            h, m = divmod(j, 2)
            q = _head(q_ref, h)
            keep = lane < DIFF_DIM if m == 0 else lane >= DIFF_DIM
            qc = jnp.where(keep, q, jnp.zeros_like(q))
            bias = _bias_block(tab_ref, h, (qi * TQ - chunks[t] * TKF) // LANES, DIFF_FIRST, DIFF_TILES)
            return bias + lax.dot_general(k_ref[0, rows[t], _head_cols(h)], qc, _NT, preferred_element_type=F32)

        def update(i, s):
            t, j = divmod(i, 2 * N_HEADS)
            _softmax_step(s, vt_ref[_head_cols(j // 2), rows[t]], m_ref, acc_ref, j)

        _pipelined(len(chunks) * 2 * N_HEADS, logits, update)

    _for_chunk_groups(0, _last_key_tile(qi, TKF) + 1, run)

    lqk = lqk_ref[...]
    lam = (jnp.exp(jnp.sum(lqk[0:1] * lqk[1:2], axis=1, keepdims=True))
           - jnp.exp(jnp.sum(lqk[2:3] * lqk[3:4], axis=1, keepdims=True)) + lambda_init)
    for h in range(N_HEADS):
        o = _normalized(acc_ref, 2 * h) - lam * _normalized(acc_ref, 2 * h + 1)
        y = o * lax.rsqrt(jnp.mean(o * o, axis=0, keepdims=True) + RMS_EPS) * sub_ref[...]
        _store_head(o_ref, h, y * (1.0 - lambda_init))


def _diff_attention(pack, vt, tab, lqk, subln_w, lambda_init):
    bsz, seq, _ = pack.shape
    sub = jnp.broadcast_to(subln_w[:, None], (HEAD_DIM, TQ))
    return pl.pallas_call(
        functools.partial(_diff_kernel, lambda_init),
        grid=(bsz, seq // TQ),
        in_specs=_seq_specs(MIX_C, seq)
        + [pl.BlockSpec(tab.shape, lambda b, i: (0, 0, 0, 0)),
           pl.BlockSpec(lqk.shape, lambda b, i: (0, 0)),
           pl.BlockSpec((HEAD_DIM, TQ), lambda b, i: (0, 0))],
        out_specs=pl.BlockSpec((1, TQ, BRANCH_WIDTH), lambda b, i: (b, i, 0)),
        out_shape=jax.ShapeDtypeStruct((bsz, seq, BRANCH_WIDTH), F32),
        scratch_shapes=_softmax_scratch(2 * N_HEADS),
        compiler_params=_cparams(("parallel", "arbitrary")),
        name="diff_attention",
    )(pack, pack, vt, tab, lqk, sub)


DIL_FIRST = -1
DIL_TILES = MAX_DISTANCE // LANES + 1 - DIL_FIRST + 1


def _dil_kernel(q_ref, k_ref, vt_ref, tab_ref, o_ref, m_ref, acc_ref):
    qi = pl.program_id(1)
    _init_softmax(m_ref, acc_ref)

    first = jnp.maximum(qi * TQ - MAX_DISTANCE, 0) // TKF
    last = _last_key_tile(qi, TKF)

    def run(chunks):
        rows = [_key_rows(c, TKF) for c in chunks]

        def logits(i):
            t, h = divmod(i, N_HEADS)
            bias = _bias_block(tab_ref, h, (qi * TQ - chunks[t] * TKF) // LANES, DIL_FIRST, DIL_TILES)
            return bias + lax.dot_general(k_ref[0, rows[t], _head_cols(h)], _head(q_ref, h), _NT,
                                          preferred_element_type=F32)

        def update(i, s):
            t, h = divmod(i, N_HEADS)
            _softmax_step(s, vt_ref[_head_cols(h), rows[t]], m_ref, acc_ref, h)

        _pipelined(len(chunks) * N_HEADS, logits, update)

    _for_chunk_groups(first, last - first + 1, run)
    for h in range(N_HEADS):
        _store_head(o_ref, h, _normalized(acc_ref, h))


def _dil_attention(pack, vt, tab):
    bsz, seq, _ = pack.shape
    return pl.pallas_call(
        _dil_kernel,
        grid=(bsz, seq // TQ),
        in_specs=_seq_specs(MIX_D, seq) + [pl.BlockSpec(tab.shape, lambda b, i: (0, 0, 0, 0))],
        out_specs=pl.BlockSpec((1, TQ, BRANCH_WIDTH), lambda b, i: (b, i, 0)),
        out_shape=jax.ShapeDtypeStruct((bsz, seq, BRANCH_WIDTH), F32),
        scratch_shapes=_softmax_scratch(N_HEADS),
        compiler_params=_cparams(("parallel", "arbitrary")),
        name="dilated_attention",
    )(pack, pack, vt, tab)


DSA_FIRST, DSA_TILES = -1, 15
LOWEST = -3.0e38
ACC_ROWS = 4 * SUBLANES
COARSE_ITERS = 10
BF16_HALF_ULP = 2.0 ** -8
TINY = 1e-30
MAX_FINE_ITERS = 400


def _fold(x, op):
    return op(x.reshape(TKF // ACC_ROWS, ACC_ROWS, TQ), axis=0)


def _dsa_kernel(topk, index_steps, iq_ref, wt_ref, k2_ref, q_ref, k_ref, vt_ref, tab_ref, o_ref,
                sc_ref, scb_ref, thr_ref, m_ref, acc_ref):
    qi = pl.program_id(1)
    lane = lax.broadcasted_iota(jnp.int32, (TQ, 2 * IDX_DIM), 1)
    wt = wt_ref[0] * (IDX_HEADS ** -0.5 * IDX_DIM ** -0.5)
    n_chunks = _last_key_tile(qi, TKF) + 1
    key_c = lax.broadcasted_iota(jnp.int32, (TKF, TQ), 0)
    qry_c = lax.broadcasted_iota(jnp.int32, (TKF, TQ), 1)

    chunk_rows = functools.partial(_key_rows, tk=TKF)

    def score_chunk(c, last, hi_lo):
        k2 = k2_ref[0, chunk_rows(c), :]
        acc = jnp.zeros((TKF, TQ), F32)
        for pair in range(IDX_HEADS // 2):
            qp = iq_ref[0, :, pair * 2 * IDX_DIM:(pair + 1) * 2 * IDX_DIM]
            for half in range(2):
                keep = lane < IDX_DIM if half == 0 else lane >= IDX_DIM
                qc = jnp.where(keep, qp, jnp.zeros_like(qp))
                rel = jnp.maximum(lax.dot_general(k2, qc, _NT, preferred_element_type=F32), 0.0)
                hh = 2 * pair + half
                acc = acc + wt[hh:hh + 1, :] * rel
        low = acc
        if last:
            causal = c * TKF + key_c <= qi * TQ + qry_c
            low = jnp.where(causal, acc, jnp.inf)
            acc = jnp.where(causal, acc, -jnp.inf)
        sc_ref[chunk_rows(c), :] = acc
        scb_ref[chunk_rows(c), :] = acc.astype(BF16)
        return jnp.maximum(hi_lo[0], _fold(acc, jnp.max)), jnp.minimum(hi_lo[1], _fold(low, jnp.min))

    hi_lo = (jnp.full((ACC_ROWS, TQ), -jnp.inf, F32), jnp.full((ACC_ROWS, TQ), jnp.inf, F32))
    hi_lo = lax.fori_loop(0, n_chunks - 1, lambda c, st: score_chunk(c, False, st), hi_lo)
    hi_lo = score_chunk(n_chunks - 1, True, hi_lo)
    row_max = jnp.max(hi_lo[0], axis=0, keepdims=True)
    row_min = jnp.min(hi_lo[1], axis=0, keepdims=True)

    def reduce_tiles(fn, init, reduce, combine):
        def body(c, carry):
            return combine(carry, _fold(fn(sc_ref[chunk_rows(c), :], c), reduce))
        return reduce(lax.fori_loop(0, n_chunks, body, init), axis=0, keepdims=True)

    def count(pred):
        return reduce_tiles(lambda blk, kt: jnp.where(pred(blk, kt), 1.0, 0.0),
                            jnp.zeros((ACC_ROWS, TQ), F32), jnp.sum, jnp.add)

    t_q = qi * TQ + lax.broadcasted_iota(jnp.int32, (1, TQ), 1)
    few = t_q < topk
    kf = float(topk)

    def count_b(mid_b):
        def body(c, carry):
            blk = scb_ref[chunk_rows(c), :]
            ind = jnp.where(blk >= mid_b, jnp.ones_like(blk), jnp.zeros_like(blk))
            parts = ind.reshape(TKF // ACC_ROWS, ACC_ROWS, TQ)
            for g in range(TKF // ACC_ROWS):
                carry = carry + parts[g]
            return carry
        acc = lax.fori_loop(0, n_chunks, body, jnp.zeros((ACC_ROWS, TQ), BF16))
        return jnp.sum(acc.astype(F32), axis=0, keepdims=True)

    def coarse(_, st):
        lo, hi = st
        mid_b = (lo + (hi - lo) * 0.5).astype(BF16)
        ge = count_b(mid_b) >= kf
        mid = mid_b.astype(F32)
        return jnp.where(ge, mid, lo), jnp.where(ge, hi, mid)

    lo_b = row_min.astype(BF16).astype(F32)
    hi_b = (row_max + (jnp.abs(row_max) * (4 * BF16_HALF_ULP) + TINY)).astype(BF16).astype(F32)
    lo_b, hi_b = lax.fori_loop(0, COARSE_ITERS, coarse, (lo_b, hi_b))
    lo0 = lo_b - (jnp.abs(lo_b) * BF16_HALF_ULP + TINY)
    hi0 = hi_b + (jnp.abs(hi_b) * BF16_HALF_ULP + TINY)

    def cond(st):
        return jnp.logical_and(st[0] < MAX_FINE_ITERS, jnp.min(st[4]) < 0.5)

    def body(st):
        it, lo, hi, thr, done, tie = st
        mid = lo + (hi - lo) * 0.5
        cnt = count(lambda blk, kt: blk >= mid)
        live = done < 0.5
        exact = jnp.logical_and(live, cnt == kf)
        stuck = jnp.logical_and(jnp.logical_and(live, cnt != kf), jnp.logical_or(mid <= lo, mid >= hi))
        thr = jnp.where(exact, mid, jnp.where(stuck, lo, thr))
        tie = jnp.where(stuck, 1.0, tie)
        done = jnp.where(jnp.logical_or(exact, stuck), 1.0, done)
        lo = jnp.where(cnt > kf, mid, lo)
        hi = jnp.where(cnt < kf, mid, hi)
        return it + 1, lo, hi, thr, done, tie

    init = (jnp.int32(0), lo0, hi0,
            jnp.full((1, TQ), LOWEST, F32), jnp.where(few, 1.0, 0.0), jnp.zeros((1, TQ), F32))
    _, _, _, thr, _, tie = lax.while_loop(cond, body, init)
    thr_ref[...] = jnp.broadcast_to(thr, thr_ref.shape)

    @pl.when(jnp.max(tie) > 0.5)
    def _():
        is_tie = tie > 0.5
        need = kf - count(lambda blk, kt: blk > thr)

        def kept(limit):
            return count(lambda blk, c: jnp.logical_and(blk == thr, c * TKF + key_c <= limit))

        def jbody(_, st):
            jlo, jhi = st
            jmid = (jlo + jhi) // 2
            ok = kept(jmid) >= need
            return jnp.where(ok, jlo, jmid), jnp.where(ok, jmid, jhi)

        jlo0 = jnp.full((1, TQ), -1, jnp.int32)
        jhi0 = jnp.zeros((1, TQ), jnp.int32) + (n_chunks * TKF - 1)
        _, cut = lax.fori_loop(0, index_steps, jbody, (jlo0, jhi0))

        def drop(c, carry):
            blk = sc_ref[chunk_rows(c), :]
            extra = jnp.logical_and(jnp.logical_and(is_tie, blk == thr), c * TKF + key_c > cut)
            sc_ref[chunk_rows(c), :] = jnp.where(extra, -jnp.inf, blk)
            return carry

        lax.fori_loop(0, n_chunks, drop, 0)

    _init_softmax(m_ref, acc_ref)

    def attend(chunks):
        rows = [chunk_rows(c) for c in chunks]
        sel = [sc_ref[r, :] >= thr_ref[0:1, :] for r in rows]

        def logits(i):
            t, h = divmod(i, N_HEADS)
            bias = _bias_block(tab_ref, h, (qi * TQ - chunks[t] * TKF) // LANES, DSA_FIRST, DSA_TILES)
            return bias + lax.dot_general(k_ref[0, rows[t], _head_cols(h)], _head(q_ref, h), _NT,
                                          preferred_element_type=F32)

        def update(i, s):
            t, h = divmod(i, N_HEADS)
            _softmax_step(jnp.where(sel[t], s, NEG), vt_ref[_head_cols(h), rows[t]], m_ref, acc_ref, h)

        _pipelined(len(chunks) * N_HEADS, logits, update)

    _for_chunk_groups(0, n_chunks, attend, group=3)
    for h in range(N_HEADS):
        _store_head(o_ref, h, _normalized(acc_ref, h))


def _dsa_attention(pack, vt, w_t, k2, tab):
    bsz, seq, _ = pack.shape
    nq = seq // TQ
    topk = min(TOPK_MAX, seq // 4)
    qb = _q_block(MIX_A)
    return pl.pallas_call(
        functools.partial(_dsa_kernel, topk, (seq - 1).bit_length() + 1),
        grid=(bsz, nq),
        in_specs=[pl.BlockSpec((1, TQ, IDX_HEADS * IDX_DIM), lambda b, i: (b, i, 0)),
                  pl.BlockSpec((1, IDX_HEADS, TQ), lambda b, i: (b, 0, i)),
                  pl.BlockSpec((1, seq, 2 * IDX_DIM), lambda b, i: (b, 0, 0), pipeline_mode=_ONCE_PER_BATCH),
                  pl.BlockSpec((1, TQ, BRANCH_WIDTH), lambda b, i: (b, i, qb)),
                  pl.BlockSpec((1, seq, BRANCH_WIDTH), lambda b, i: (b, 0, qb + 1), pipeline_mode=_ONCE_PER_BATCH),
                  pl.BlockSpec((BRANCH_WIDTH, seq), lambda b, i: (MIX_A, b), pipeline_mode=_ONCE_PER_BATCH),
                  pl.BlockSpec(tab.shape, lambda b, i: (0, 0, 0, 0))],
        out_specs=pl.BlockSpec((1, TQ, BRANCH_WIDTH), lambda b, i: (b, i, 0)),
        out_shape=jax.ShapeDtypeStruct((bsz, seq, BRANCH_WIDTH), F32),
        scratch_shapes=[pltpu.VMEM((seq, TQ), F32), pltpu.VMEM((seq, TQ), BF16), pltpu.VMEM((SUBLANES, TQ), F32)]
        + _softmax_scratch(N_HEADS),
        compiler_params=_cparams(("parallel", "arbitrary")),
        name="dsa_attention",
    )(pack, w_t, k2, pack, pack, vt, tab)


def _merge_kernel(oa_ref, ob_ref, oc_ref, od_ref, z_ref, g0_ref, g1_ref, g2_ref, g3_ref, wb_ref, o_ref):
    merged = None
    for b, (o_ref_b, g_ref) in enumerate(zip((oa_ref, ob_ref, oc_ref, od_ref),
                                             (g0_ref, g1_ref, g2_ref, g3_ref))):
        zh = z_ref[:, b * BRANCH_WIDTH:(b + 1) * BRANCH_WIDTH].astype(F32)
        y = o_ref_b[...] * (zh * jnp.tanh(zh) + zh)
        half = jnp.dot(y.astype(BF16), wb_ref[b], preferred_element_type=F32)
        term = jnp.tanh(g_ref[...].astype(F32)) * half + half
        merged = term if merged is None else merged + term
    o_ref[...] = merged.astype(o_ref.dtype)


def _merge(branches, zg, w_branch, tm=256):
    m = zg.shape[0]
    o_spec = pl.BlockSpec((tm, BRANCH_WIDTH), lambda i: (i, 0))
    g_specs = [pl.BlockSpec((tm, D_MODEL), functools.partial(lambda i, b: (i, 1 + b), b=b))
               for b in range(N_BRANCHES)]
    return pl.pallas_call(
        _merge_kernel,
        grid=(m // tm,),
        in_specs=[o_spec] * N_BRANCHES + [pl.BlockSpec((tm, N_BRANCHES * BRANCH_WIDTH), lambda i: (i, 0))]
        + g_specs + [pl.BlockSpec(w_branch.shape, lambda i: (0, 0, 0))],
        out_specs=pl.BlockSpec((tm, D_MODEL), lambda i: (i, 0)),
        out_shape=jax.ShapeDtypeStruct((m, D_MODEL), BF16),
        compiler_params=_cparams(("parallel",)),
        name="gated_merge",
    )(*branches, zg, zg, zg, zg, zg, w_branch)


def _pack_w_in(w_in):
    sizes = (N_BRANCHES * BRANCH_WIDTH, N_BRANCHES * D_MODEL, 3 * BRANCH_WIDTH, IDX_HEADS * IDX_DIM,
             IDX_DIM, IDX_HEADS, 3 * BRANCH_WIDTH, N_HEADS, 3 * BRANCH_WIDTH, 3 * BRANCH_WIDTH)
    offs = np.concatenate([[0], np.cumsum(sizes)])
    w = w_in.astype(BF16)
    z, gate, a, iq, ik, iw, b, ff, c, d = [w[:, :, offs[n]:offs[n + 1]] for n in range(len(sizes))]
    qk = [m[:, :, :2 * BRANCH_WIDTH] for m in (a, b, c, d)]
    v = [m[:, :, 2 * BRANCH_WIDTH:] for m in (a, b, c, d)]
    pad = jnp.zeros(w.shape[:2] + (N_MISC - MISC_F - N_HEADS,), w.dtype)
    half = jnp.asarray(0.5, BF16)
    return (jnp.concatenate([iq] + qk, axis=-1), jnp.concatenate(v, axis=-1),
            jnp.concatenate([z * half, gate * half], axis=-1), jnp.concatenate([ik, ik, iw, ff, pad], axis=-1))


def _pack_col_scale():
    s = np.ones((1, N_PACK), np.float32)
    for mixer, sc in enumerate((HEAD_DIM ** -0.5, HEAD_DIM ** -0.5, DIFF_DIM ** -0.5, HEAD_DIM ** -0.5)):
        col = COL_QK + mixer * 2 * BRANCH_WIDTH
        s[:, col:col + BRANCH_WIDTH] = sc * LOG2E
    return jnp.asarray(s)


def _layer(x2, bsz, seq, layer, norm_w, w_pack, w_v, w_zg, w_misc, fox_b_f, lqk, subln_w, w_branch, w_out,
           tab_dsa, tab_diff, tab_dil):
    h = _rmsnorm(x2, norm_w, BF16)
    pack = _matmul(h, w_pack, BF16, PROJ_TM, PROJ_TN, col_scale=_pack_col_scale(),
                   name="proj_pack").reshape(bsz, seq, N_PACK)
    vt = _matmul(h, w_v, BF16, PROJ_TM, PROJ_TN, transpose_out=True, name="proj_vt")
    zg = _matmul(h, w_zg, BF16, PROJ_TM, PROJ_TN, name="proj_zg")
    misc = _matmul(h, w_misc, F32, PROJ_TM, N_MISC, name="proj_misc").reshape(bsz, seq, N_MISC)

    k2 = misc[:, :, :MISC_W].astype(BF16)
    w_t = jnp.transpose(misc[:, :, MISC_W:MISC_F], (0, 2, 1))
    o_dsa = _dsa_attention(pack, vt, w_t, k2, tab_dsa)

    f_bsh = misc[:, :, MISC_F:MISC_F + N_HEADS]
    cum_bhs = _fox_cum(jnp.transpose(f_bsh, (0, 2, 1)), fox_b_f)
    o_fox = _fox_attention(pack, vt, jnp.transpose(cum_bhs, (0, 2, 1)), cum_bhs)

    lambda_init = 0.8 - 0.6 * math.exp(-0.3 * layer)
    o_diff = _diff_attention(pack, vt, tab_diff, lqk, subln_w, lambda_init)
    o_dil = _dil_attention(pack, vt, tab_dil)

    m = bsz * seq
    branches = [o.reshape(m, BRANCH_WIDTH) for o in (o_dsa, o_fox, o_diff, o_dil)]
    merged = _merge(branches, zg, w_branch)
    return _matmul(merged, w_out, F32, PROJ_TM, PROJ_TN, residual=x2, name="out_proj")


def kernel(x, norm_w, w_in, fox_b_f, diff_lq1, diff_lk1, diff_lq2, diff_lk2, diff_subln_w, w_branch, w_out,
           rel_bias, final_norm_w):
    bsz, seq, d = x.shape
    w_pack, w_v, w_zg, w_misc = _pack_w_in(w_in)
    w_branch_bf = (w_branch * 0.5).astype(BF16)
    w_out_bf = w_out.astype(BF16)
    lqk = jnp.stack([diff_lq1, diff_lk1, diff_lq2, diff_lk2], axis=1)
    bias_dsa, bias_diff, bias_dil = jnp.split(rel_bias, 3, axis=-1)
    tab_dsa = _bias_tiles(bias_dsa, DSA_FIRST, DSA_TILES, causal=False)
    tab_diff = _bias_tiles(bias_diff, DIFF_FIRST, DIFF_TILES, causal=True)
    tab_dil = _dilated_tiles(bias_dil, DIL_FIRST, DIL_TILES)

    x2 = x.reshape(bsz * seq, d)
    for layer in range(DEPTH):
        x2 = _layer(x2, bsz, seq, layer, norm_w[layer], w_pack[layer], w_v[layer], w_zg[layer], w_misc[layer],
                    fox_b_f[layer], lqk[layer], diff_subln_w[layer], w_branch_bf[layer], w_out_bf[layer],
                    tab_dsa, tab_diff, tab_dil)
    return _rmsnorm(x2, final_norm_w, F32).reshape(bsz, seq, d)
```

```python
import functools
import math

import jax
import jax.numpy as jnp
import numpy as np
from jax import lax
from jax.experimental import pallas as pl
from jax.experimental.pallas import tpu as pltpu

F32 = jnp.float32
BF16 = jnp.bfloat16

D_MODEL = 2048
DEPTH = 4
HEAD_DIM = 128
N_HEADS = 4
BRANCH_WIDTH = N_HEADS * HEAD_DIM
N_BRANCHES = 4
DIFF_DIM = HEAD_DIM // 2
IDX_HEADS = 16
IDX_DIM = 64
TOPK_MAX = 256
DILATED_CONFIGS = ((128, 1), (512, 4), (2048, 16))
N_BUCKETS = 32
MAX_DISTANCE = 2048
RMS_EPS = 1e-6

LANES = 128
SUBLANES = 8
ONES_ROWS = 16
TQ = 256
TKF = 512
LOG2E = math.log2(math.e)
PROJ_TM, PROJ_TN = 2048, 512
NEG = -1e30
VMEM_LIMIT = 56 * 1024 * 1024

COL_IDXQ = 0
COL_QK = IDX_HEADS * IDX_DIM
N_PACK = COL_QK + N_BRANCHES * 2 * BRANCH_WIDTH
N_V = N_BRANCHES * BRANCH_WIDTH
N_ZG = N_BRANCHES * BRANCH_WIDTH + N_BRANCHES * D_MODEL
N_MISC = 256
MISC_W = 2 * IDX_DIM
MISC_F = MISC_W + IDX_HEADS
MIX_A, MIX_B, MIX_C, MIX_D = range(N_BRANCHES)

_NT = (((1,), (1,)), ((), ()))
_ONCE_PER_BATCH = pl.Buffered(1)


def _cparams(sem):
    return pltpu.CompilerParams(dimension_semantics=sem, vmem_limit_bytes=VMEM_LIMIT)


def _q_block(mixer):
    return COL_QK // BRANCH_WIDTH + 2 * mixer


def _rmsnorm_kernel(x_ref, w_ref, o_ref):
    x = x_ref[...]
    y = x * lax.rsqrt(jnp.mean(x * x, axis=-1, keepdims=True) + RMS_EPS)
    o_ref[...] = (y * w_ref[...]).astype(o_ref.dtype)


def _rmsnorm(x2, w, out_dtype, tm=512):
    m, d = x2.shape
    return pl.pallas_call(
        _rmsnorm_kernel,
        grid=(m // tm,),
        in_specs=[pl.BlockSpec((tm, d), lambda i: (i, 0)), pl.BlockSpec((1, d), lambda i: (0, 0))],
        out_specs=pl.BlockSpec((tm, d), lambda i: (i, 0)),
        out_shape=jax.ShapeDtypeStruct((m, d), out_dtype),
        compiler_params=_cparams(("parallel",)),
        name="rmsnorm",
    )(x2, w.reshape(1, d))


def _matmul_kernel(a_ref, b_ref, o_ref):
    o_ref[...] = jnp.dot(a_ref[...], b_ref[...], preferred_element_type=F32).astype(o_ref.dtype)


def _matmul_t_kernel(a_ref, b_ref, o_ref):
    o_ref[...] = jnp.dot(a_ref[...], b_ref[...], preferred_element_type=F32).T.astype(o_ref.dtype)


def _matmul_res_kernel(a_ref, b_ref, r_ref, o_ref):
    o_ref[...] = r_ref[...] + jnp.dot(a_ref[...], b_ref[...], preferred_element_type=F32)


def _matmul_scale_kernel(a_ref, b_ref, s_ref, o_ref):
    o_ref[...] = (jnp.dot(a_ref[...], b_ref[...], preferred_element_type=F32) * s_ref[...]).astype(o_ref.dtype)


def _matmul(a, b, out_dtype, tm, tn, residual=None, col_scale=None, transpose_out=False, name="matmul"):
    m, k = a.shape
    n = b.shape[1]
    tm, tn = min(tm, m), min(tn, n)
    in_specs = [pl.BlockSpec((tm, k), lambda i, j: (i, 0)), pl.BlockSpec((k, tn), lambda i, j: (0, j))]
    args = [a, b]
    kern = _matmul_kernel
    out_spec = pl.BlockSpec((tm, tn), lambda i, j: (i, j))
    out_shape = (m, n)
    if residual is not None:
        in_specs.append(pl.BlockSpec((tm, tn), lambda i, j: (i, j)))
        args.append(residual)
        kern = _matmul_res_kernel
    if col_scale is not None:
        in_specs.append(pl.BlockSpec((1, tn), lambda i, j: (0, j)))
        args.append(col_scale)
        kern = _matmul_scale_kernel
    if transpose_out:
        kern = _matmul_t_kernel
        out_spec = pl.BlockSpec((tn, tm), lambda i, j: (j, i))
        out_shape = (n, m)
    return pl.pallas_call(
        kern,
        grid=(m // tm, n // tn),
        in_specs=in_specs,
        out_specs=out_spec,
        out_shape=jax.ShapeDtypeStruct(out_shape, out_dtype),
        compiler_params=_cparams(("parallel", "arbitrary")),
        name=name,
    )(*args)


def _t5_bucket(dist):
    dist = jnp.maximum(dist, 0)
    max_exact = N_BUCKETS // 2
    d = jnp.maximum(dist, max_exact).astype(F32)
    large = max_exact + (jnp.log(d / max_exact) / math.log(MAX_DISTANCE / max_exact)
                         * (N_BUCKETS - max_exact)).astype(jnp.int32)
    large = jnp.minimum(large, N_BUCKETS - 1)
    return jnp.where(dist < max_exact, dist, large)


def _toeplitz_tiles(vals, n_tiles):
    nh, length = vals.shape
    hank = jnp.tile(vals, (1, LANES + 1))[:, :LANES * (length + 1)].reshape(nh, LANES, length + 1)
    hank = hank[:, ::-1, :n_tiles * LANES]
    return jnp.transpose(hank.reshape(nh, LANES, n_tiles, LANES), (0, 2, 1, 3))


def _distances(first, n_tiles):
    return first * LANES - (LANES - 1) + jnp.arange(n_tiles * LANES + LANES - 1)


def _bias_tiles(table, first, n_tiles, causal):
    d = _distances(first, n_tiles)
    vals = table[_t5_bucket(d)].T
    if causal:
        vals = jnp.where(d[None] >= 0, vals, NEG)
    return _toeplitz_tiles(vals * LOG2E, n_tiles)


def _dilated_tiles(table, first, n_tiles):
    d = _distances(first, n_tiles)
    mult = jnp.zeros(d.shape, F32)
    for window, dilation in DILATED_CONFIGS:
        mult = mult + ((d >= 0) & (d <= window) & (d % dilation == 0)).astype(F32)
    vals = table[_t5_bucket(d)].T + jnp.log(jnp.maximum(mult, 1.0))[None]
    return _toeplitz_tiles(jnp.where(mult[None] > 0, vals, NEG) * LOG2E, n_tiles)


def _bias_block(tab_ref, h, a, first, n_tiles):
    def tile(off):
        return tab_ref[h, jnp.clip(off - first, 0, n_tiles - 1)]
    rows = [jnp.concatenate([tile(a + ib - jb) for ib in range(TQ // LANES)], axis=1)
            for jb in range(TKF // LANES)]
    return jnp.concatenate(rows, axis=0)


def _softmax_step(s, vt, m_ref, acc_ref, idx):
    m_prev = m_ref[idx]
    m_next = jnp.maximum(m_prev, jnp.max(s, axis=0, keepdims=True))
    alpha = jnp.exp2(m_prev - m_next)
    p = jnp.exp2(s - m_next).astype(BF16)
    vt_ones = jnp.concatenate([vt, jnp.ones((ONES_ROWS, vt.shape[1]), vt.dtype)], axis=0)
    m_ref[idx] = m_next
    acc_ref[idx] = acc_ref[idx] * alpha + jnp.dot(vt_ones, p, preferred_element_type=F32)


def _normalized(acc_ref, idx):
    acc = acc_ref[idx]
    return acc[:HEAD_DIM] / acc[HEAD_DIM:HEAD_DIM + 1]


def _pipelined(n, logits_fn, update_fn, lookahead=4):
    pending = {i: logits_fn(i) for i in range(min(lookahead, n))}
    for i in range(n):
        if i + lookahead < n:
            pending[i + lookahead] = logits_fn(i + lookahead)
        update_fn(i, pending.pop(i))


def _for_chunk_groups(first, count, run, group=2):
    def body(p, carry):
        run([first + group * p + g for g in range(group)])
        return carry

    lax.fori_loop(0, count // group, body, 0)

    def rest(c, carry):
        run([first + c])
        return carry

    lax.fori_loop(count - count % group, count, rest, 0)


def _init_softmax(m_ref, acc_ref):
    m_ref[...] = jnp.full(m_ref.shape, NEG, F32)
    acc_ref[...] = jnp.zeros(acc_ref.shape, F32)


def _softmax_scratch(n):
    return [pltpu.VMEM((n, 1, TQ), F32), pltpu.VMEM((n, HEAD_DIM + ONES_ROWS, TQ), F32)]


def _head(ref, h):
    return ref[0, :, h * HEAD_DIM:(h + 1) * HEAD_DIM]


def _store_head(o_ref, h, o_t):
    o_ref[0, :, h * HEAD_DIM:(h + 1) * HEAD_DIM] = o_t.T


def _logsig_cumsum_kernel(f_ref, b_ref, o_ref):
    x = f_ref[0, 0] + b_ref[0, 0]
    ls = jnp.minimum(x, 0.0) - jnp.log1p(jnp.exp(-jnp.abs(x)))
    rows = ls.shape[0]
    r = lax.broadcasted_iota(jnp.int32, (LANES, LANES), 0)
    c = lax.broadcasted_iota(jnp.int32, (LANES, LANES), 1)
    upper = (r <= c).astype(F32)
    within = jnp.dot(ls, upper, preferred_element_type=F32, precision=lax.Precision.HIGHEST)
    total = jnp.broadcast_to(within[:, LANES - 1:LANES], (rows, LANES))
    rr = lax.broadcasted_iota(jnp.int32, (rows, rows), 0)
    rc = lax.broadcasted_iota(jnp.int32, (rows, rows), 1)
    strict = (rc < rr).astype(F32)
    before = jnp.dot(strict, total, preferred_element_type=F32, precision=lax.Precision.HIGHEST)
    o_ref[0, 0] = within + before


def _fox_cum(f_bhs, b_f):
    bsz, nh, seq = f_bhs.shape
    rows = seq // LANES
    out = pl.pallas_call(
        _logsig_cumsum_kernel,
        grid=(bsz, nh),
        in_specs=[pl.BlockSpec((1, 1, rows, LANES), lambda b, h: (b, h, 0, 0)),
                  pl.BlockSpec((1, 1, 1, LANES), lambda b, h: (0, h, 0, 0))],
        out_specs=pl.BlockSpec((1, 1, rows, LANES), lambda b, h: (b, h, 0, 0)),
        out_shape=jax.ShapeDtypeStruct((bsz, nh, rows, LANES), F32),
        compiler_params=_cparams(("parallel", "parallel")),
        name="fox_cumsum",
    )(f_bhs.reshape(bsz, nh, rows, LANES), jnp.broadcast_to(b_f[None, :, None, None], (1, nh, 1, LANES)))
    return out.reshape(bsz, nh, seq)


def _last_key_tile(qi, tk):
    return (qi * TQ + TQ - 1) // tk


def _key_rows(c, tk):
    return pl.ds(pl.multiple_of(c * tk, tk), tk)


def _head_cols(h):
    return slice(h * HEAD_DIM, (h + 1) * HEAD_DIM)


def _seq_specs(mixer, seq):
    qb = _q_block(mixer)
    return [pl.BlockSpec((1, TQ, BRANCH_WIDTH), lambda b, i: (b, i, qb)),
            pl.BlockSpec((1, seq, BRANCH_WIDTH), lambda b, i: (b, 0, qb + 1), pipeline_mode=_ONCE_PER_BATCH),
            pl.BlockSpec((BRANCH_WIDTH, seq), lambda b, i: (mixer, b), pipeline_mode=_ONCE_PER_BATCH)]


def _fox_kernel(q_ref, k_ref, vt_ref, cq_ref, ck_ref, o_ref, m_ref, acc_ref):
    qi = pl.program_id(1)
    last = _last_key_tile(qi, TKF)
    _init_softmax(m_ref, acc_ref)

    def run(chunks, diagonal=False):
        rows = [_key_rows(c, TKF) for c in chunks]
        if diagonal:
            key = chunks[0] * TKF + lax.broadcasted_iota(jnp.int32, (TKF, TQ), 0)
            causal = key <= qi * TQ + lax.broadcasted_iota(jnp.int32, (TKF, TQ), 1)

        def logits(i):
            t, h = divmod(i, N_HEADS)
            decay = cq_ref[0, h:h + 1, :] * LOG2E - ck_ref[0, rows[t], h:h + 1] * LOG2E
            return decay + lax.dot_general(k_ref[0, rows[t], _head_cols(h)], _head(q_ref, h), _NT,
                                           preferred_element_type=F32)

        def update(i, s):
            t, h = divmod(i, N_HEADS)
            if diagonal:
                s = jnp.where(causal, s, NEG)
            _softmax_step(s, vt_ref[_head_cols(h), rows[t]], m_ref, acc_ref, h)

        _pipelined(len(chunks) * N_HEADS, logits, update)

    _for_chunk_groups(0, last, run, group=3)
    run([last], diagonal=True)
    for h in range(N_HEADS):
        _store_head(o_ref, h, _normalized(acc_ref, h))


def _fox_attention(pack, vt, cum_bsh, cum_bhs):
    bsz, seq, _ = pack.shape
    return pl.pallas_call(
        _fox_kernel,
        grid=(bsz, seq // TQ),
        in_specs=_seq_specs(MIX_B, seq)
        + [pl.BlockSpec((1, N_HEADS, TQ), lambda b, i: (b, 0, i)),
           pl.BlockSpec((1, seq, N_HEADS), lambda b, i: (b, 0, 0))],
        out_specs=pl.BlockSpec((1, TQ, BRANCH_WIDTH), lambda b, i: (b, i, 0)),
        out_shape=jax.ShapeDtypeStruct((bsz, seq, BRANCH_WIDTH), F32),
        scratch_shapes=_softmax_scratch(N_HEADS),
        compiler_params=_cparams(("parallel", "arbitrary")),
        name="fox_attention",
    )(pack, pack, vt, cum_bhs, cum_bsh)


DIFF_FIRST, DIFF_TILES = -1, 15


def _diff_kernel(lambda_init, q_ref, k_ref, vt_ref, tab_ref, lqk_ref, sub_ref, o_ref, m_ref, acc_ref):
    qi = pl.program_id(1)
    _init_softmax(m_ref, acc_ref)
    lane = lax.broadcasted_iota(jnp.int32, (TQ, HEAD_DIM), 1)

    def run(chunks):
        rows = [_key_rows(c, TKF) for c in chunks]

        def logits(i):
            t, j = divmod(i, 2 * N_HEADS)
            h, m = divmod(j, 2)
            q = _head(q_ref, h)
            keep = lane < DIFF_DIM if m == 0 else lane >= DIFF_DIM
            qc = jnp.where(keep, q, jnp.zeros_like(q))
            bias = _bias_block(tab_ref, h, (qi * TQ - chunks[t] * TKF) // LANES, DIFF_FIRST, DIFF_TILES)
            return bias + lax.dot_general(k_ref[0, rows[t], _head_cols(h)], qc, _NT, preferred_element_type=F32)

        def update(i, s):
            t, j = divmod(i, 2 * N_HEADS)
            _softmax_step(s, vt_ref[_head_cols(j // 2), rows[t]], m_ref, acc_ref, j)

        _pipelined(len(chunks) * 2 * N_HEADS, logits, update)

    _for_chunk_groups(0, _last_key_tile(qi, TKF) + 1, run, group=3)

    lqk = lqk_ref[...]
    lam = (jnp.exp(jnp.sum(lqk[0:1] * lqk[1:2], axis=1, keepdims=True))
           - jnp.exp(jnp.sum(lqk[2:3] * lqk[3:4], axis=1, keepdims=True)) + lambda_init)
    for h in range(N_HEADS):
        o = _normalized(acc_ref, 2 * h) - lam * _normalized(acc_ref, 2 * h + 1)
        y = o * lax.rsqrt(jnp.mean(o * o, axis=0, keepdims=True) + RMS_EPS) * sub_ref[...]
        _store_head(o_ref, h, y * (1.0 - lambda_init))


def _diff_attention(pack, vt, tab, lqk, subln_w, lambda_init):
    bsz, seq, _ = pack.shape
    sub = jnp.broadcast_to(subln_w[:, None], (HEAD_DIM, TQ))
    return pl.pallas_call(
        functools.partial(_diff_kernel, lambda_init),
        grid=(bsz, seq // TQ),
        in_specs=_seq_specs(MIX_C, seq)
        + [pl.BlockSpec(tab.shape, lambda b, i: (0, 0, 0, 0)),
           pl.BlockSpec(lqk.shape, lambda b, i: (0, 0)),
           pl.BlockSpec((HEAD_DIM, TQ), lambda b, i: (0, 0))],
        out_specs=pl.BlockSpec((1, TQ, BRANCH_WIDTH), lambda b, i: (b, i, 0)),
        out_shape=jax.ShapeDtypeStruct((bsz, seq, BRANCH_WIDTH), F32),
        scratch_shapes=_softmax_scratch(2 * N_HEADS),
        compiler_params=_cparams(("parallel", "arbitrary")),
        name="diff_attention",
    )(pack, pack, vt, tab, lqk, sub)


DIL_FIRST = -1
DIL_TILES = MAX_DISTANCE // LANES + 1 - DIL_FIRST + 1


def _dil_kernel(q_ref, k_ref, vt_ref, tab_ref, o_ref, m_ref, acc_ref):
    qi = pl.program_id(1)
    _init_softmax(m_ref, acc_ref)

    first = jnp.maximum(qi * TQ - MAX_DISTANCE, 0) // TKF
    last = _last_key_tile(qi, TKF)

    def run(chunks):
        rows = [_key_rows(c, TKF) for c in chunks]

        def logits(i):
            t, h = divmod(i, N_HEADS)
            bias = _bias_block(tab_ref, h, (qi * TQ - chunks[t] * TKF) // LANES, DIL_FIRST, DIL_TILES)
            return bias + lax.dot_general(k_ref[0, rows[t], _head_cols(h)], _head(q_ref, h), _NT,
                                          preferred_element_type=F32)

        def update(i, s):
            t, h = divmod(i, N_HEADS)
            _softmax_step(s, vt_ref[_head_cols(h), rows[t]], m_ref, acc_ref, h)

        _pipelined(len(chunks) * N_HEADS, logits, update)

    _for_chunk_groups(first, last - first + 1, run)
    for h in range(N_HEADS):
        _store_head(o_ref, h, _normalized(acc_ref, h))


def _dil_attention(pack, vt, tab):
    bsz, seq, _ = pack.shape
    return pl.pallas_call(
        _dil_kernel,
        grid=(bsz, seq // TQ),
        in_specs=_seq_specs(MIX_D, seq) + [pl.BlockSpec(tab.shape, lambda b, i: (0, 0, 0, 0))],
        out_specs=pl.BlockSpec((1, TQ, BRANCH_WIDTH), lambda b, i: (b, i, 0)),
        out_shape=jax.ShapeDtypeStruct((bsz, seq, BRANCH_WIDTH), F32),
        scratch_shapes=_softmax_scratch(N_HEADS),
        compiler_params=_cparams(("parallel", "arbitrary")),
        name="dilated_attention",
    )(pack, pack, vt, tab)


DSA_FIRST, DSA_TILES = -1, 15
LOWEST = -3.0e38
ACC_ROWS = 4 * SUBLANES
COARSE_ITERS = 10
BF16_HALF_ULP = 2.0 ** -8
TINY = 1e-30
MAX_FINE_ITERS = 400


def _fold(x, op):
    return op(x.reshape(TKF // ACC_ROWS, ACC_ROWS, TQ), axis=0)


def _dsa_kernel(topk, index_steps, iq_ref, wt_ref, k2_ref, q_ref, k_ref, vt_ref, tab_ref, o_ref,
                sc_ref, scb_ref, thr_ref, m_ref, acc_ref):
    qi = pl.program_id(1)
    lane = lax.broadcasted_iota(jnp.int32, (TQ, 2 * IDX_DIM), 1)
    wt = wt_ref[0] * (IDX_HEADS ** -0.5 * IDX_DIM ** -0.5)
    n_chunks = _last_key_tile(qi, TKF) + 1
    key_c = lax.broadcasted_iota(jnp.int32, (TKF, TQ), 0)
    qry_c = lax.broadcasted_iota(jnp.int32, (TKF, TQ), 1)

    chunk_rows = functools.partial(_key_rows, tk=TKF)

    def score_chunk(c, last, hi_lo):
        k2 = k2_ref[0, chunk_rows(c), :]
        acc = jnp.zeros((TKF, TQ), F32)
        for pair in range(IDX_HEADS // 2):
            qp = iq_ref[0, :, pair * 2 * IDX_DIM:(pair + 1) * 2 * IDX_DIM]
            for half in range(2):
                keep = lane < IDX_DIM if half == 0 else lane >= IDX_DIM
                qc = jnp.where(keep, qp, jnp.zeros_like(qp))
                rel = jnp.maximum(lax.dot_general(k2, qc, _NT, preferred_element_type=F32), 0.0)
                hh = 2 * pair + half
                acc = acc + wt[hh:hh + 1, :] * rel
        low = acc
        if last:
            causal = c * TKF + key_c <= qi * TQ + qry_c
            low = jnp.where(causal, acc, jnp.inf)
            acc = jnp.where(causal, acc, -jnp.inf)
        sc_ref[chunk_rows(c), :] = acc
        scb_ref[chunk_rows(c), :] = acc.astype(BF16)
        return jnp.maximum(hi_lo[0], _fold(acc, jnp.max)), jnp.minimum(hi_lo[1], _fold(low, jnp.min))

    hi_lo = (jnp.full((ACC_ROWS, TQ), -jnp.inf, F32), jnp.full((ACC_ROWS, TQ), jnp.inf, F32))
    def score_pair(p, st):
        return score_chunk(2 * p + 1, False, score_chunk(2 * p, False, st))

    hi_lo = lax.fori_loop(0, (n_chunks - 1) // 2, score_pair, hi_lo)
    hi_lo = lax.fori_loop(2 * ((n_chunks - 1) // 2), n_chunks - 1, lambda c, st: score_chunk(c, False, st), hi_lo)
    hi_lo = score_chunk(n_chunks - 1, True, hi_lo)
    row_max = jnp.max(hi_lo[0], axis=0, keepdims=True)
    row_min = jnp.min(hi_lo[1], axis=0, keepdims=True)

    def reduce_tiles(fn, init, reduce, combine):
        def body(c, carry):
            return combine(carry, _fold(fn(sc_ref[chunk_rows(c), :], c), reduce))
        return reduce(lax.fori_loop(0, n_chunks, body, init), axis=0, keepdims=True)

    def count(pred):
        return reduce_tiles(lambda blk, kt: jnp.where(pred(blk, kt), 1.0, 0.0),
                            jnp.zeros((ACC_ROWS, TQ), F32), jnp.sum, jnp.add)

    t_q = qi * TQ + lax.broadcasted_iota(jnp.int32, (1, TQ), 1)
    few = t_q < topk
    kf = float(topk)

    def count_b(mid_b):
        def body(c, carry):
            blk = scb_ref[chunk_rows(c), :]
            ind = jnp.where(blk >= mid_b, jnp.ones_like(blk), jnp.zeros_like(blk))
            parts = ind.reshape(TKF // ACC_ROWS, ACC_ROWS, TQ)
            for g in range(TKF // ACC_ROWS):
                carry = carry + parts[g]
            return carry
        acc = lax.fori_loop(0, n_chunks, body, jnp.zeros((ACC_ROWS, TQ), BF16))
        return jnp.sum(acc.astype(F32), axis=0, keepdims=True)

    def coarse(_, st):
        lo, hi = st
        mid_b = (lo + (hi - lo) * 0.5).astype(BF16)
        ge = count_b(mid_b) >= kf
        mid = mid_b.astype(F32)
        return jnp.where(ge, mid, lo), jnp.where(ge, hi, mid)

    lo_b = row_min.astype(BF16).astype(F32)
    hi_b = (row_max + (jnp.abs(row_max) * (4 * BF16_HALF_ULP) + TINY)).astype(BF16).astype(F32)
    lo_b, hi_b = lax.fori_loop(0, COARSE_ITERS, coarse, (lo_b, hi_b))
    lo0 = lo_b - (jnp.abs(lo_b) * BF16_HALF_ULP + TINY)
    hi0 = hi_b + (jnp.abs(hi_b) * BF16_HALF_ULP + TINY)

    def cond(st):
        return jnp.logical_and(st[0] < MAX_FINE_ITERS, jnp.min(st[4]) < 0.5)

    def body(st):
        it, lo, hi, thr, done, tie = st
        mid = lo + (hi - lo) * 0.5
        cnt = count(lambda blk, kt: blk >= mid)
        live = done < 0.5
        exact = jnp.logical_and(live, cnt == kf)
        stuck = jnp.logical_and(jnp.logical_and(live, cnt != kf), jnp.logical_or(mid <= lo, mid >= hi))
        thr = jnp.where(exact, mid, jnp.where(stuck, lo, thr))
        tie = jnp.where(stuck, 1.0, tie)
        done = jnp.where(jnp.logical_or(exact, stuck), 1.0, done)
        lo = jnp.where(cnt > kf, mid, lo)
        hi = jnp.where(cnt < kf, mid, hi)
        return it + 1, lo, hi, thr, done, tie

    init = (jnp.int32(0), lo0, hi0,
            jnp.full((1, TQ), LOWEST, F32), jnp.where(few, 1.0, 0.0), jnp.zeros((1, TQ), F32))
    _, _, _, thr, _, tie = lax.while_loop(cond, body, init)
    thr_ref[...] = jnp.broadcast_to(thr, thr_ref.shape)

    @pl.when(jnp.max(tie) > 0.5)
    def _():
        is_tie = tie > 0.5
        need = kf - count(lambda blk, kt: blk > thr)

        def kept(limit):
            return count(lambda blk, c: jnp.logical_and(blk == thr, c * TKF + key_c <= limit))

        def jbody(_, st):
            jlo, jhi = st
            jmid = (jlo + jhi) // 2
            ok = kept(jmid) >= need
            return jnp.where(ok, jlo, jmid), jnp.where(ok, jmid, jhi)

        jlo0 = jnp.full((1, TQ), -1, jnp.int32)
        jhi0 = jnp.zeros((1, TQ), jnp.int32) + (n_chunks * TKF - 1)
        _, cut = lax.fori_loop(0, index_steps, jbody, (jlo0, jhi0))

        def drop(c, carry):
            blk = sc_ref[chunk_rows(c), :]
            extra = jnp.logical_and(jnp.logical_and(is_tie, blk == thr), c * TKF + key_c > cut)
            sc_ref[chunk_rows(c), :] = jnp.where(extra, -jnp.inf, blk)
            return carry

        lax.fori_loop(0, n_chunks, drop, 0)

    _init_softmax(m_ref, acc_ref)

    def attend(chunks):
        rows = [chunk_rows(c) for c in chunks]
        sel = [sc_ref[r, :] >= thr_ref[0:1, :] for r in rows]

        def logits(i):
            t, h = divmod(i, N_HEADS)
            bias = _bias_block(tab_ref, h, (qi * TQ - chunks[t] * TKF) // LANES, DSA_FIRST, DSA_TILES)
            return bias + lax.dot_general(k_ref[0, rows[t], _head_cols(h)], _head(q_ref, h), _NT,
                                          preferred_element_type=F32)

        def update(i, s):
            t, h = divmod(i, N_HEADS)
            _softmax_step(jnp.where(sel[t], s, NEG), vt_ref[_head_cols(h), rows[t]], m_ref, acc_ref, h)

        _pipelined(len(chunks) * N_HEADS, logits, update)

    _for_chunk_groups(0, n_chunks, attend, group=3)
    for h in range(N_HEADS):
        _store_head(o_ref, h, _normalized(acc_ref, h))


def _dsa_attention(pack, vt, w_t, k2, tab):
    bsz, seq, _ = pack.shape
    nq = seq // TQ
    topk = min(TOPK_MAX, seq // 4)
    qb = _q_block(MIX_A)
    return pl.pallas_call(
        functools.partial(_dsa_kernel, topk, (seq - 1).bit_length() + 1),
        grid=(bsz, nq),
        in_specs=[pl.BlockSpec((1, TQ, IDX_HEADS * IDX_DIM), lambda b, i: (b, i, 0)),
                  pl.BlockSpec((1, IDX_HEADS, TQ), lambda b, i: (b, 0, i)),
                  pl.BlockSpec((1, seq, 2 * IDX_DIM), lambda b, i: (b, 0, 0), pipeline_mode=_ONCE_PER_BATCH),
                  pl.BlockSpec((1, TQ, BRANCH_WIDTH), lambda b, i: (b, i, qb)),
                  pl.BlockSpec((1, seq, BRANCH_WIDTH), lambda b, i: (b, 0, qb + 1), pipeline_mode=_ONCE_PER_BATCH),
                  pl.BlockSpec((BRANCH_WIDTH, seq), lambda b, i: (MIX_A, b), pipeline_mode=_ONCE_PER_BATCH),
                  pl.BlockSpec(tab.shape, lambda b, i: (0, 0, 0, 0))],
        out_specs=pl.BlockSpec((1, TQ, BRANCH_WIDTH), lambda b, i: (b, i, 0)),
        out_shape=jax.ShapeDtypeStruct((bsz, seq, BRANCH_WIDTH), F32),
        scratch_shapes=[pltpu.VMEM((seq, TQ), F32), pltpu.VMEM((seq, TQ), BF16), pltpu.VMEM((SUBLANES, TQ), F32)]
        + _softmax_scratch(N_HEADS),
        compiler_params=_cparams(("parallel", "arbitrary")),
        name="dsa_attention",
    )(pack, w_t, k2, pack, pack, vt, tab)


def _merge_kernel(oa_ref, ob_ref, oc_ref, od_ref, z_ref, g0_ref, g1_ref, g2_ref, g3_ref, wb_ref, o_ref):
    merged = None
    for b, (o_ref_b, g_ref) in enumerate(zip((oa_ref, ob_ref, oc_ref, od_ref),
                                             (g0_ref, g1_ref, g2_ref, g3_ref))):
        zh = z_ref[:, b * BRANCH_WIDTH:(b + 1) * BRANCH_WIDTH].astype(F32)
        y = o_ref_b[...] * (zh * jnp.tanh(zh) + zh)
        half = jnp.dot(y.astype(BF16), wb_ref[b], preferred_element_type=F32)
        term = jnp.tanh(g_ref[...].astype(F32)) * half + half
        merged = term if merged is None else merged + term
    o_ref[...] = merged.astype(o_ref.dtype)


def _merge(branches, zg, w_branch, tm=512):
    m = zg.shape[0]
    o_spec = pl.BlockSpec((tm, BRANCH_WIDTH), lambda i: (i, 0))
    g_specs = [pl.BlockSpec((tm, D_MODEL), functools.partial(lambda i, b: (i, 1 + b), b=b))
               for b in range(N_BRANCHES)]
    return pl.pallas_call(
        _merge_kernel,
        grid=(m // tm,),
        in_specs=[o_spec] * N_BRANCHES + [pl.BlockSpec((tm, N_BRANCHES * BRANCH_WIDTH), lambda i: (i, 0))]
        + g_specs + [pl.BlockSpec(w_branch.shape, lambda i: (0, 0, 0))],
        out_specs=pl.BlockSpec((tm, D_MODEL), lambda i: (i, 0)),
        out_shape=jax.ShapeDtypeStruct((m, D_MODEL), BF16),
        compiler_params=_cparams(("parallel",)),
        name="gated_merge",
    )(*branches, zg, zg, zg, zg, zg, w_branch)


def _pack_w_in(w_in):
    sizes = (N_BRANCHES * BRANCH_WIDTH, N_BRANCHES * D_MODEL, 3 * BRANCH_WIDTH, IDX_HEADS * IDX_DIM,
             IDX_DIM, IDX_HEADS, 3 * BRANCH_WIDTH, N_HEADS, 3 * BRANCH_WIDTH, 3 * BRANCH_WIDTH)
    offs = np.concatenate([[0], np.cumsum(sizes)])
    w = w_in.astype(BF16)
    z, gate, a, iq, ik, iw, b, ff, c, d = [w[:, :, offs[n]:offs[n + 1]] for n in range(len(sizes))]
    qk = [m[:, :, :2 * BRANCH_WIDTH] for m in (a, b, c, d)]
    v = [m[:, :, 2 * BRANCH_WIDTH:] for m in (a, b, c, d)]
    pad = jnp.zeros(w.shape[:2] + (N_MISC - MISC_F - N_HEADS,), w.dtype)
    half = jnp.asarray(0.5, BF16)
    return (jnp.concatenate([iq] + qk, axis=-1), jnp.concatenate(v, axis=-1),
            jnp.concatenate([z * half, gate * half], axis=-1), jnp.concatenate([ik, ik, iw, ff, pad], axis=-1))


def _pack_col_scale():
    s = np.ones((1, N_PACK), np.float32)
    for mixer, sc in enumerate((HEAD_DIM ** -0.5, HEAD_DIM ** -0.5, DIFF_DIM ** -0.5, HEAD_DIM ** -0.5)):
        col = COL_QK + mixer * 2 * BRANCH_WIDTH
        s[:, col:col + BRANCH_WIDTH] = sc * LOG2E
    return jnp.asarray(s)


def _layer(x2, bsz, seq, layer, norm_w, w_pack, w_v, w_zg, w_misc, fox_b_f, lqk, subln_w, w_branch, w_out,
           tab_dsa, tab_diff, tab_dil):
    h = _rmsnorm(x2, norm_w, BF16)
    pack = _matmul(h, w_pack, BF16, PROJ_TM, PROJ_TN, col_scale=_pack_col_scale(),
                   name="proj_pack").reshape(bsz, seq, N_PACK)
    vt = _matmul(h, w_v, BF16, PROJ_TM, PROJ_TN, transpose_out=True, name="proj_vt")
    zg = _matmul(h, w_zg, BF16, PROJ_TM, PROJ_TN, name="proj_zg")
    misc = _matmul(h, w_misc, F32, PROJ_TM, N_MISC, name="proj_misc").reshape(bsz, seq, N_MISC)

    k2 = misc[:, :, :MISC_W].astype(BF16)
    w_t = jnp.transpose(misc[:, :, MISC_W:MISC_F], (0, 2, 1))
    o_dsa = _dsa_attention(pack, vt, w_t, k2, tab_dsa)

    f_bsh = misc[:, :, MISC_F:MISC_F + N_HEADS]
    cum_bhs = _fox_cum(jnp.transpose(f_bsh, (0, 2, 1)), fox_b_f)
    o_fox = _fox_attention(pack, vt, jnp.transpose(cum_bhs, (0, 2, 1)), cum_bhs)

    lambda_init = 0.8 - 0.6 * math.exp(-0.3 * layer)
    o_diff = _diff_attention(pack, vt, tab_diff, lqk, subln_w, lambda_init)
    o_dil = _dil_attention(pack, vt, tab_dil)

    m = bsz * seq
    branches = [o.reshape(m, BRANCH_WIDTH) for o in (o_dsa, o_fox, o_diff, o_dil)]
    merged = _merge(branches, zg, w_branch)
    return _matmul(merged, w_out, F32, PROJ_TM, PROJ_TN, residual=x2, name="out_proj")


def kernel(x, norm_w, w_in, fox_b_f, diff_lq1, diff_lk1, diff_lq2, diff_lk2, diff_subln_w, w_branch, w_out,
           rel_bias, final_norm_w):
    bsz, seq, d = x.shape
    w_pack, w_v, w_zg, w_misc = _pack_w_in(w_in)
    w_branch_bf = (w_branch * 0.5).astype(BF16)
    w_out_bf = w_out.astype(BF16)
    lqk = jnp.stack([diff_lq1, diff_lk1, diff_lq2, diff_lk2], axis=1)
    bias_dsa, bias_diff, bias_dil = jnp.split(rel_bias, 3, axis=-1)
    tab_dsa = _bias_tiles(bias_dsa, DSA_FIRST, DSA_TILES, causal=False)
    tab_diff = _bias_tiles(bias_diff, DIFF_FIRST, DIFF_TILES, causal=True)
    tab_dil = _dilated_tiles(bias_dil, DIL_FIRST, DIL_TILES)

    x2 = x.reshape(bsz * seq, d)
    for layer in range(DEPTH):
        x2 = _layer(x2, bsz, seq, layer, norm_w[layer], w_pack[layer], w_v[layer], w_zg[layer], w_misc[layer],
                    fox_b_f[layer], lqk[layer], diff_subln_w[layer], w_branch_bf[layer], w_out_bf[layer],
                    tab_dsa, tab_diff, tab_dil)
    return _rmsnorm(x2, final_norm_w, F32).reshape(bsz, seq, d)
```
